```python
import math
import jax, jax.numpy as jnp
from jax import lax
import numpy as np

D_MODEL = 1024
BATCH = 8
SEQ = 4096
DEPTH = 1

DIFF_HEADS = 4
DIFF_HEAD_DIM = 64
DIFF_V_DIM = 2 * DIFF_HEAD_DIM
DIFF_QK_COLS = DIFF_HEADS * 2 * DIFF_HEAD_DIM
DIFF_V_COLS = DIFF_HEADS * DIFF_V_DIM
MLA_HEADS = 4
MLA_Q_RANK = 384
MLA_KV_RANK = 256
MLA_NOPE_DIM = 128
MLA_ROPE_DIM = 64
MLA_V_DIM = 128
MLA_V_COLS = MLA_HEADS * MLA_V_DIM
ROPE_BASE = 10000.0
IN_COLS = 2 * DIFF_QK_COLS + DIFF_V_COLS + MLA_Q_RANK + MLA_KV_RANK + MLA_ROPE_DIM
MIX_WIDTH = DIFF_V_COLS + MLA_V_COLS
D_FF = 4 * D_MODEL
REL_BUCKETS = 32
REL_MAX_DIST = 128
Q_BLOCK = 128
NORM_EPS = 1e-6
NEG_INF = -1e30

kernel_name = "hymba_diffattn_mla_sqrelu_block"


def rms_norm(x, gain):
    x32 = x.astype(jnp.float32)
    y = x32 * lax.rsqrt(jnp.mean(x32 * x32, axis=-1, keepdims=True) + NORM_EPS)
    return (y * gain.astype(jnp.float32)).astype(x.dtype)


def t5_bucket(dist):
    n = jnp.maximum(dist, 0)
    max_exact = REL_BUCKETS // 2
    nf = jnp.maximum(n, 1).astype(jnp.float32)
    large = max_exact + (jnp.log(nf / max_exact) / math.log(REL_MAX_DIST / max_exact)
                         * (REL_BUCKETS - max_exact)).astype(jnp.int32)
    large = jnp.minimum(large, REL_BUCKETS - 1)
    return jnp.where(n < max_exact, n, large)


def apply_rope(x, positions):
    inv_freq = ROPE_BASE ** (-jnp.arange(0, MLA_ROPE_DIM, 2, dtype=jnp.float32) / MLA_ROPE_DIM)
    ang = positions.astype(jnp.float32)[:, None] * inv_freq[None, :]
    cos, sin = jnp.cos(ang), jnp.sin(ang)
    x32 = x.astype(jnp.float32)
    x1, x2 = jnp.split(x32, 2, axis=-1)
    out = jnp.concatenate([x1 * cos - x2 * sin, x1 * sin + x2 * cos], axis=-1)
    return out.astype(x.dtype)


def diff_attention(q, k, v, positions, rel_bias, lam, lambda_init, subln):
    B, S = q.shape[0], q.shape[1]
    nb = S // Q_BLOCK
    scale = DIFF_HEAD_DIM ** -0.5
    qb = q.reshape(B, nb, Q_BLOCK, DIFF_HEADS, 2, DIFF_HEAD_DIM).transpose(1, 0, 3, 4, 2, 5)
    kt = k.transpose(0, 2, 3, 1, 4)
    vt = v.transpose(0, 2, 1, 3)
    pos_blocks = positions.reshape(nb, Q_BLOCK)

    def block(args):
        q_blk, q_pos = args
        logits = jnp.einsum('bhcqd,bhckd->bhcqk', q_blk, kt,
                            preferred_element_type=jnp.float32) * scale
        dist = q_pos[:, None] - positions[None, :]
        bias = jnp.transpose(rel_bias[t5_bucket(dist)], (2, 0, 1))
        logits = logits + bias.astype(jnp.float32)[None, :, None]
        logits = jnp.where((dist >= 0)[None, None, None], logits, NEG_INF)
        probs = jax.nn.softmax(logits, axis=-1)
        attn = probs[:, :, 0] - lam * probs[:, :, 1]
        return jnp.einsum('bhqk,bhkd->bhqd', attn.astype(vt.dtype), vt)

    out = lax.map(block, (qb, pos_blocks))
    out = out.transpose(1, 0, 3, 2, 4).reshape(B, S, DIFF_HEADS, DIFF_V_DIM)
    out = rms_norm(out, subln) * (1.0 - lambda_init)
    return out.reshape(B, S, DIFF_V_COLS)


def mla_attention(c_q, c_kv, k_pe, positions, q_norm, w_uq, kv_norm, w_ukv):
    B, S = c_q.shape[0], c_q.shape[1]
    nb = S // Q_BLOCK
    q = (rms_norm(c_q, q_norm) @ w_uq).reshape(B, S, MLA_HEADS, MLA_NOPE_DIM + MLA_ROPE_DIM)
    q = q.transpose(0, 2, 1, 3)
    q_nope, q_pe = q[..., :MLA_NOPE_DIM], q[..., MLA_NOPE_DIM:]
    q_pe = apply_rope(q_pe, positions)
    kv = (rms_norm(c_kv, kv_norm) @ w_ukv).reshape(B, S, MLA_HEADS, MLA_NOPE_DIM + MLA_V_DIM)
    kv = kv.transpose(0, 2, 1, 3)
    k_nope, v = kv[..., :MLA_NOPE_DIM], kv[..., MLA_NOPE_DIM:]
    k_pe = apply_rope(k_pe, positions)
    scale = (MLA_NOPE_DIM + MLA_ROPE_DIM) ** -0.5
    qn_b = q_nope.reshape(B, MLA_HEADS, nb, Q_BLOCK, MLA_NOPE_DIM).transpose(2, 0, 1, 3, 4)
    qp_b = q_pe.reshape(B, MLA_HEADS, nb, Q_BLOCK, MLA_ROPE_DIM).transpose(2, 0, 1, 3, 4)
    pos_blocks = positions.reshape(nb, Q_BLOCK)

    def block(args):
        qn, qp, q_pos = args
        logits = (jnp.einsum('bhqd,bhkd->bhqk', qn, k_nope, preferred_element_type=jnp.float32)
                  + jnp.einsum('bhqr,bkr->bhqk', qp, k_pe, preferred_element_type=jnp.float32)) * scale
        mask = positions[None, :] <= q_pos[:, None]
        logits = jnp.where(mask[None, None], logits, NEG_INF)
        probs = jax.nn.softmax(logits, axis=-1)
        return jnp.einsum('bhqk,bhkd->bhqd', probs.astype(v.dtype), v)

    out = lax.map(block, (qn_b, qp_b, pos_blocks))
    return out.transpose(1, 0, 3, 2, 4).reshape(B, S, MLA_V_COLS)


def setup_inputs(seed: int = 0) -> dict:
    key = jax.random.key(seed)
    ks = jax.random.split(key, 24)
    f32 = jnp.float32

    def nrm(k, shape, scale):
        return jax.random.normal(k, shape, f32) * scale

    def gain(k, shape):
        return 1.0 + 0.02 * jax.random.normal(k, shape, f32)

    return {
        "x": jax.random.normal(ks[0], (BATCH, SEQ, D_MODEL), f32),
        "positions": jnp.arange(SEQ, dtype=jnp.int32),
        "rel_bias": nrm(ks[1], (REL_BUCKETS, DIFF_HEADS), 0.1),
        "norm_attn": gain(ks[2], (DEPTH, D_MODEL)),
        "w_in": nrm(ks[3], (DEPTH, D_MODEL, IN_COLS), D_MODEL ** -0.5),
        "diff_lq1": nrm(ks[4], (DEPTH, DIFF_HEAD_DIM), 0.1),
        "diff_lk1": nrm(ks[5], (DEPTH, DIFF_HEAD_DIM), 0.1),
        "diff_lq2": nrm(ks[6], (DEPTH, DIFF_HEAD_DIM), 0.1),
        "diff_lk2": nrm(ks[7], (DEPTH, DIFF_HEAD_DIM), 0.1),
        "diff_subln": gain(ks[8], (DEPTH, DIFF_V_DIM)),
        "mla_q_norm": gain(ks[9], (DEPTH, MLA_Q_RANK)),
        "mla_w_uq": nrm(ks[10], (DEPTH, MLA_Q_RANK, MLA_HEADS * (MLA_NOPE_DIM + MLA_ROPE_DIM)), MLA_Q_RANK ** -0.5),
        "mla_kv_norm": gain(ks[11], (DEPTH, MLA_KV_RANK)),
        "mla_w_ukv": nrm(ks[12], (DEPTH, MLA_KV_RANK, MLA_HEADS * (MLA_NOPE_DIM + MLA_V_DIM)), MLA_KV_RANK ** -0.5),
        "w_out": nrm(ks[13], (DEPTH, MIX_WIDTH, D_MODEL), MIX_WIDTH ** -0.5),
        "norm_mlp": gain(ks[14], (DEPTH, D_MODEL)),
        "w_mlp_in": nrm(ks[15], (DEPTH, D_MODEL, D_FF), D_MODEL ** -0.5),
        "w_mlp_out": nrm(ks[16], (DEPTH, D_FF, D_MODEL), D_FF ** -0.5),
        "norm_final": gain(ks[17], (D_MODEL,)),
    }


def reference(x, positions, rel_bias, norm_attn, w_in, diff_lq1, diff_lk1, diff_lq2, diff_lk2,
              diff_subln, mla_q_norm, mla_w_uq, mla_kv_norm, mla_w_ukv, w_out, norm_mlp,
              w_mlp_in, w_mlp_out, norm_final):
    B, S = x.shape[0], x.shape[1]
    splits = [DIFF_QK_COLS, 2 * DIFF_QK_COLS, 2 * DIFF_QK_COLS + DIFF_V_COLS,
              2 * DIFF_QK_COLS + DIFF_V_COLS + MLA_Q_RANK,
              2 * DIFF_QK_COLS + DIFF_V_COLS + MLA_Q_RANK + MLA_KV_RANK]
    for l in range(DEPTH):
        lambda_init = 0.8 - 0.6 * math.exp(-0.3 * l)
        h = rms_norm(x, norm_attn[l])
        proj = h @ w_in[l]
        dq, dk, dv, c_q, c_kv, k_pe = jnp.split(proj, splits, axis=-1)
        dq = dq.reshape(B, S, DIFF_HEADS, 2, DIFF_HEAD_DIM)
        dk = dk.reshape(B, S, DIFF_HEADS, 2, DIFF_HEAD_DIM)
        dv = dv.reshape(B, S, DIFF_HEADS, DIFF_V_DIM)
        lam = (jnp.exp(jnp.sum(diff_lq1[l].astype(jnp.float32) * diff_lk1[l].astype(jnp.float32)))
               - jnp.exp(jnp.sum(diff_lq2[l].astype(jnp.float32) * diff_lk2[l].astype(jnp.float32)))
               + lambda_init)
        out_a = diff_attention(dq, dk, dv, positions, rel_bias, lam, lambda_init, diff_subln[l])
        out_b = mla_attention(c_q, c_kv, k_pe, positions, mla_q_norm[l], mla_w_uq[l],
                              mla_kv_norm[l], mla_w_ukv[l])
        x = x + jnp.concatenate([out_a, out_b], axis=-1) @ w_out[l]
        h = rms_norm(x, norm_mlp[l])
        x = x + jnp.square(jax.nn.relu(h @ w_mlp_in[l])) @ w_mlp_out[l]
    return rms_norm(x, norm_final)
```

```python
import functools
import math

import jax
import jax.numpy as jnp
from jax import lax
from jax.experimental import pallas as pl
from jax.experimental.pallas import tpu as pltpu

D_MODEL = 1024
DIFF_HEADS = 4
DIFF_HEAD_DIM = 64
DIFF_V_DIM = 128
DIFF_QK_COLS = DIFF_HEADS * 2 * DIFF_HEAD_DIM
DIFF_V_COLS = DIFF_HEADS * DIFF_V_DIM
MLA_HEADS = 4
MLA_Q_RANK = 384
MLA_KV_RANK = 256
MLA_NOPE_DIM = 128
MLA_ROPE_DIM = 64
MLA_V_DIM = 128
MLA_QK_PAD = 256
ROPE_BASE = 10000.0
D_FF = 4 * D_MODEL
REL_BUCKETS = 32
REL_MAX_DIST = 128
NORM_EPS = 1e-6
NEG_INF = -1e30
LOG2E = math.log2(math.e)
LAMBDA_INIT = 0.8 - 0.6 * math.exp(-0.3 * 0)

LANES = 128
VMEM_LIMIT = 48 * 1024 * 1024

NT_DIMS = (((1,), (1,)), ((), ()))


def _rms(x, gain):
    return x * lax.rsqrt(jnp.mean(x * x, axis=-1, keepdims=True) + NORM_EPS) * gain


def _rope_table_kernel(pos_ref, tab_ref):
    half = MLA_ROPE_DIM // 2
    lane = lax.broadcasted_iota(jnp.int32, (1, LANES), 1)
    fidx = (lane % half).astype(jnp.float32) * 2.0
    inv_freq = jnp.exp(-(fidx / MLA_ROPE_DIM) * math.log(ROPE_BASE))
    ang = pos_ref[...].astype(jnp.float32) * inv_freq
    c, s = jnp.cos(ang), jnp.sin(ang)
    tab_ref[...] = jnp.where(lane < 2 * half, c, jnp.where(lane < 3 * half, -s, s))


def _rope_table(positions, tile):
    seq = positions.shape[0]
    return pl.pallas_call(
        _rope_table_kernel,
        grid=(seq // tile,),
        in_specs=[pl.BlockSpec((tile, 1), lambda i: (i, 0))],
        out_specs=pl.BlockSpec((tile, LANES), lambda i: (i, 0)),
        out_shape=jax.ShapeDtypeStruct((seq, LANES), jnp.float32),
        name="rope_table",
    )(positions.reshape(seq, 1))


def _rope_slab(slab, tab):
    prod = slab * tab
    return prod + pltpu.roll(prod, MLA_ROPE_DIM, axis=1)


def _proj_kernel(x_ref, g_ref, w_in_ref, qn_ref, w_uq_ref, kvn_ref, w_ukv_ref, tab_ref,
                 dq_ref, dk_ref, dv_ref, mq_ref, mk_ref, mv_ref):
    bf = jnp.bfloat16
    h = _rms(x_ref[0], g_ref[...]).astype(bf)
    p = jnp.dot(h, w_in_ref[...], preferred_element_type=jnp.float32)
    tab = tab_ref[...]
    lane = lax.broadcasted_iota(jnp.int32, tab.shape, 1)

    dscale = (DIFF_HEAD_DIM ** -0.5) * LOG2E
    o_k, o_v = DIFF_QK_COLS, 2 * DIFF_QK_COLS
    for hh in range(DIFF_HEADS):
        sl = slice(hh * LANES, (hh + 1) * LANES)
        dq_ref[0, hh] = (p[:, sl] * dscale).astype(bf)
        dk_ref[0, hh] = p[:, o_k + hh * LANES:o_k + (hh + 1) * LANES].astype(bf)
        dv_ref[0, hh] = p[:, o_v + hh * LANES:o_v + (hh + 1) * LANES].astype(bf)

    o_cq = 2 * DIFF_QK_COLS + DIFF_V_COLS
    o_ckv = o_cq + MLA_Q_RANK
    o_kpe = o_ckv + MLA_KV_RANK
    cq = _rms(p[:, o_cq:o_ckv], qn_ref[...]).astype(bf)
    q = jnp.dot(cq, w_uq_ref[...], preferred_element_type=jnp.float32)
    ckv = _rms(p[:, o_ckv:o_kpe], kvn_ref[...]).astype(bf)
    kv = jnp.dot(ckv, w_ukv_ref[...], preferred_element_type=jnp.float32)
    k_rope = _rope_slab(p[:, o_kpe:o_kpe + LANES], tab).astype(bf)

    mscale = ((MLA_NOPE_DIM + MLA_ROPE_DIM) ** -0.5) * LOG2E
    for hh in range(MLA_HEADS):
        base = hh * MLA_QK_PAD
        q_rope = _rope_slab(q[:, base + LANES:base + 2 * LANES], tab)
        q_rope = jnp.where(lane < MLA_ROPE_DIM, q_rope, 0.0)
        mq_ref[0, hh, :, 0:LANES] = (q[:, base:base + LANES] * mscale).astype(bf)
        mq_ref[0, hh, :, LANES:2 * LANES] = (q_rope * mscale).astype(bf)
        mk_ref[0, hh, :, 0:LANES] = kv[:, base:base + LANES].astype(bf)
        mk_ref[0, hh, :, LANES:2 * LANES] = k_rope
        mv_ref[0, hh] = kv[:, base + LANES:base + 2 * LANES].astype(bf)


def _proj(x, g, w_in, qn, w_uq, kvn, w_ukv, tab, tile):
    B, S, D = x.shape
    bf = jnp.bfloat16
    const = lambda b, i: (0, 0)
    head_out = lambda w: pl.BlockSpec((1, DIFF_HEADS, tile, w), lambda b, i: (b, 0, i, 0))
    shp = lambda w: jax.ShapeDtypeStruct((B, DIFF_HEADS, S, w), bf)
    return pl.pallas_call(
        _proj_kernel,
        grid=(B, S // tile),
        in_specs=[
            pl.BlockSpec((1, tile, D), lambda b, i: (b, i, 0)),
            pl.BlockSpec(g.shape, const),
            pl.BlockSpec(w_in.shape, const),
            pl.BlockSpec(qn.shape, const),
            pl.BlockSpec(w_uq.shape, const),
            pl.BlockSpec(kvn.shape, const),
            pl.BlockSpec(w_ukv.shape, const),
            pl.BlockSpec((tile, LANES), lambda b, i: (i, 0)),
        ],
        out_specs=[head_out(LANES), head_out(LANES), head_out(LANES),
                   head_out(MLA_QK_PAD), head_out(MLA_QK_PAD), head_out(LANES)],
        out_shape=[shp(LANES), shp(LANES), shp(LANES),
                   shp(MLA_QK_PAD), shp(MLA_QK_PAD), shp(LANES)],
        compiler_params=pltpu.CompilerParams(
            dimension_semantics=("arbitrary", "arbitrary"), vmem_limit_bytes=VMEM_LIMIT),
        name="proj",
    )(x, g, w_in, qn, w_uq, kvn, w_ukv, tab)


def _schedule(positions, tq, tk, far):
    seq = positions.shape[0]
    qp = positions.reshape(seq // tq, tq)
    kp = positions.reshape(seq // tk, tk)
    qmin, qmax = qp.min(axis=1), qp.max(axis=1)
    kmin, kmax = kp.min(axis=1), kp.max(axis=1)
    skip = kmin[None, :] > qmax[:, None]
    interior = (qmin[:, None] - kmax[None, :]) >= far
    cls = jnp.where(interior, 0, jnp.where(skip, 2, 1)).astype(jnp.int32)
    order = jnp.argsort(cls, axis=1, stable=True).astype(jnp.int32)
    n_int = jnp.sum(cls == 0, axis=1).astype(jnp.int32)
    n_vis = jnp.sum(cls < 2, axis=1).astype(jnp.int32)
    return order.reshape(-1), n_int, n_vis


def _flash_step(qq, k, v, add, m_ref, l_ref, acc_ref):
    s = lax.dot_general(qq, k, NT_DIMS, preferred_element_type=jnp.float32) + add
    m_old = m_ref[...]
    m_new = jnp.maximum(m_old, jnp.max(s, axis=1, keepdims=True))
    alpha = jnp.exp2(m_old - m_new)
    p = jnp.exp2(s - m_new)
    l_ref[...] = alpha * l_ref[...] + jnp.sum(p, axis=1, keepdims=True)
    acc_ref[...] = alpha * acc_ref[...] + jnp.dot(
        p.astype(jnp.bfloat16), v, preferred_element_type=jnp.float32)
    m_ref[...] = m_new


def _t5_bias_row(relb_ref, head):
    n = lax.broadcasted_iota(jnp.int32, (1, LANES), 1)
    max_exact = REL_BUCKETS // 2
    nf = jnp.maximum(n, 1).astype(jnp.float32)
    large = max_exact + (jnp.log(nf / max_exact) / math.log(REL_MAX_DIST / max_exact)
                         * (REL_BUCKETS - max_exact)).astype(jnp.int32)
    large = jnp.minimum(large, REL_BUCKETS - 1)
    bucket = jnp.where(n < max_exact, n, large)
    row = jnp.zeros((1, LANES), jnp.float32)
    for b in range(REL_BUCKETS):
        row = jnp.where(bucket == b, relb_ref[b, head] * LOG2E, row)
    return row


def _diff_kernel(tbl_ref, nint_ref, nvis_ref, relb_ref,
                 q_ref, k_ref, v_ref, qpos_ref, kpos_ref,
                 lq1_ref, lk1_ref, lq2_ref, lk2_ref, subln_ref,
                 o_ref, m_ref, l_ref, acc_ref, *, tq, tk, nk):
    head = pl.program_id(1)
    i = pl.program_id(2)
    q = q_ref[0, 0]
    lane = lax.broadcasted_iota(jnp.int32, q.shape, 1)
    zero = jnp.zeros_like(q)
    qq = jnp.concatenate([jnp.where(lane < DIFF_HEAD_DIM, q, zero),
                          jnp.where(lane >= DIFF_HEAD_DIM, q, zero)], axis=0)

    m_ref[...] = jnp.full(m_ref.shape, NEG_INF, jnp.float32)
    l_ref[...] = jnp.zeros(l_ref.shape, jnp.float32)
    acc_ref[...] = jnp.zeros(acc_ref.shape, jnp.float32)

    bias_row = _t5_bias_row(relb_ref, head)
    far_bias = relb_ref[REL_BUCKETS - 1, head] * LOG2E

    def interior(t, carry):
        j = tbl_ref[i * nk + t]
        off = pl.multiple_of(j * tk, tk)
        _flash_step(qq, k_ref[0, 0, pl.ds(off, tk), :], v_ref[0, 0, pl.ds(off, tk), :],
                    far_bias, m_ref, l_ref, acc_ref)
        return carry

    def general(t, carry):
        j = tbl_ref[i * nk + t]
        off = pl.multiple_of(j * tk, tk)
        d = qpos_ref[...] - kpos_ref[j]
        idx = jnp.clip(d, 0, LANES - 1)
        tab = jnp.broadcast_to(bias_row, (tq, LANES))
        bias = jnp.concatenate(
            [jnp.take_along_axis(tab, idx[:, c * LANES:(c + 1) * LANES], axis=1)
             for c in range(tk // LANES)], axis=1)
        add = jnp.where(d >= 0, bias, NEG_INF)
        _flash_step(qq, k_ref[0, 0, pl.ds(off, tk), :], v_ref[0, 0, pl.ds(off, tk), :],
                    jnp.concatenate([add, add], axis=0), m_ref, l_ref, acc_ref)
        return carry

    n_int = nint_ref[i]
    lax.fori_loop(0, n_int, interior, 0)
    lax.fori_loop(n_int, nvis_ref[i], general, 0)

    o = acc_ref[...] / l_ref[...]
    lam = (jnp.exp(jnp.sum(lq1_ref[...] * lk1_ref[...], axis=1, keepdims=True))
           - jnp.exp(jnp.sum(lq2_ref[...] * lk2_ref[...], axis=1, keepdims=True))
           + LAMBDA_INIT)
    oo = o[:tq] - lam * o[tq:]
    o_ref[0] = (_rms(oo, subln_ref[...]) * (1.0 - LAMBDA_INIT)).astype(o_ref.dtype)


def _diff_attn(dq, dk, dv, positions, rel_bias, lq1, lk1, lq2, lk2, subln, tq, tk):
    B, H, S, _ = dq.shape
    nq, nk = S // tq, S // tk
    tbl, n_int, n_vis = _schedule(positions, tq, tk, LANES)
    row = lambda a: a.reshape(1, -1).astype(jnp.float32)
    const = lambda b, h, i, *_: (0, 0)
    grid_spec = pltpu.PrefetchScalarGridSpec(
        num_scalar_prefetch=4,
        grid=(B, H, nq),
        in_specs=[
            pl.BlockSpec((1, 1, tq, LANES), lambda b, h, i, *_: (b, h, i, 0)),
            pl.BlockSpec((1, 1, S, LANES), lambda b, h, i, *_: (b, h, 0, 0)),
            pl.BlockSpec((1, 1, S, LANES), lambda b, h, i, *_: (b, h, 0, 0)),
            pl.BlockSpec((tq, 1), lambda b, h, i, *_: (i, 0)),
            pl.BlockSpec((nk, 1, tk), lambda b, h, i, *_: (0, 0, 0)),
            pl.BlockSpec((1, DIFF_HEAD_DIM), const),
            pl.BlockSpec((1, DIFF_HEAD_DIM), const),
            pl.BlockSpec((1, DIFF_HEAD_DIM), const),
            pl.BlockSpec((1, DIFF_HEAD_DIM), const),
            pl.BlockSpec((1, DIFF_V_DIM), const),
        ],
        out_specs=pl.BlockSpec((1, tq, LANES), lambda b, h, i, *_: (b, i, h)),
        scratch_shapes=[pltpu.VMEM((2 * tq, 1), jnp.float32),
                        pltpu.VMEM((2 * tq, 1), jnp.float32),
                        pltpu.VMEM((2 * tq, DIFF_V_DIM), jnp.float32)],
    )
    return pl.pallas_call(
        functools.partial(_diff_kernel, tq=tq, tk=tk, nk=nk),
        grid_spec=grid_spec,
        out_shape=jax.ShapeDtypeStruct((B, S, H * DIFF_V_DIM), jnp.bfloat16),
        compiler_params=pltpu.CompilerParams(
            dimension_semantics=("arbitrary",) * 3, vmem_limit_bytes=VMEM_LIMIT),
        name="diff_attn",
    )(tbl, n_int, n_vis, rel_bias.astype(jnp.float32),
      dq, dk, dv, positions.reshape(S, 1), positions.reshape(nk, 1, tk),
      row(lq1), row(lk1), row(lq2), row(lk2), row(subln))


def _mla_kernel(tbl_ref, nint_ref, nvis_ref,
                q_ref, k_ref, v_ref, qpos_ref, kpos_ref,
                o_ref, m_ref, l_ref, acc_ref, *, tq, tk, nk):
    i = pl.program_id(2)
    q = q_ref[0, 0]
    m_ref[...] = jnp.full(m_ref.shape, NEG_INF, jnp.float32)
    l_ref[...] = jnp.zeros(l_ref.shape, jnp.float32)
    acc_ref[...] = jnp.zeros(acc_ref.shape, jnp.float32)

    def interior(t, carry):
        j = tbl_ref[i * nk + t]
        off = pl.multiple_of(j * tk, tk)
        _flash_step(q, k_ref[0, 0, pl.ds(off, tk), :], v_ref[0, 0, pl.ds(off, tk), :],
                    0.0, m_ref, l_ref, acc_ref)
        return carry

    def general(t, carry):
        j = tbl_ref[i * nk + t]
        off = pl.multiple_of(j * tk, tk)
        d = qpos_ref[...] - kpos_ref[j]
        add = jnp.where(d >= 0, 0.0, NEG_INF)
        _flash_step(q, k_ref[0, 0, pl.ds(off, tk), :], v_ref[0, 0, pl.ds(off, tk), :],
                    add, m_ref, l_ref, acc_ref)
        return carry

    n_int = nint_ref[i]
    lax.fori_loop(0, n_int, interior, 0)
    lax.fori_loop(n_int, nvis_ref[i], general, 0)
    o_ref[0] = (acc_ref[...] / l_ref[...]).astype(o_ref.dtype)


def _mla_attn(mq, mk, mv, positions, tq, tk):
    B, H, S, _ = mq.shape
    nq, nk = S // tq, S // tk
    tbl, n_int, n_vis = _schedule(positions, tq, tk, 0)
    grid_spec = pltpu.PrefetchScalarGridSpec(
        num_scalar_prefetch=3,
        grid=(B, H, nq),
        in_specs=[
            pl.BlockSpec((1, 1, tq, MLA_QK_PAD), lambda b, h, i, *_: (b, h, i, 0)),
            pl.BlockSpec((1, 1, S, MLA_QK_PAD), lambda b, h, i, *_: (b, h, 0, 0)),
            pl.BlockSpec((1, 1, S, MLA_V_DIM), lambda b, h, i, *_: (b, h, 0, 0)),
            pl.BlockSpec((tq, 1), lambda b, h, i, *_: (i, 0)),
            pl.BlockSpec((nk, 1, tk), lambda b, h, i, *_: (0, 0, 0)),
        ],
        out_specs=pl.BlockSpec((1, tq, LANES), lambda b, h, i, *_: (b, i, h)),
        scratch_shapes=[pltpu.VMEM((tq, 1), jnp.float32),
                        pltpu.VMEM((tq, 1), jnp.float32),
                        pltpu.VMEM((tq, MLA_V_DIM), jnp.float32)],
    )
    return pl.pallas_call(
        functools.partial(_mla_kernel, tq=tq, tk=tk, nk=nk),
        grid_spec=grid_spec,
        out_shape=jax.ShapeDtypeStruct((B, S, H * MLA_V_DIM), jnp.bfloat16),
        compiler_params=pltpu.CompilerParams(
            dimension_semantics=("arbitrary",) * 3, vmem_limit_bytes=VMEM_LIMIT),
        name="mla_attn",
    )(tbl, n_int, n_vis, mq, mk, mv, positions.reshape(S, 1), positions.reshape(nk, 1, tk))


def _mlp_kernel(x_ref, ma_ref, mb_ref, wo_ref, g1_ref, w1_ref, w2_ref, g2_ref, o_ref, *, ff_chunk):
    bf = jnp.bfloat16
    n_a = ma_ref.shape[1]
    x1 = (x_ref[...]
          + jnp.dot(ma_ref[...], wo_ref[0:n_a, :], preferred_element_type=jnp.float32)
          + jnp.dot(mb_ref[...], wo_ref[n_a:, :], preferred_element_type=jnp.float32))
    h = _rms(x1, g1_ref[...]).astype(bf)
    y = jnp.zeros_like(x1)
    for c in range(D_FF // ff_chunk):
        sl = slice(c * ff_chunk, (c + 1) * ff_chunk)
        a = jnp.dot(h, w1_ref[:, sl], preferred_element_type=jnp.float32)
        a = jnp.square(jnp.maximum(a, 0.0)).astype(bf)
        y = y + jnp.dot(a, w2_ref[sl, :], preferred_element_type=jnp.float32)
    o_ref[...] = _rms(x1 + y, g2_ref[...])


def _mlp(x2d, mix_a, mix_b, w_out, g1, w1, w2, g2, tile, ff_chunk):
    N, D = x2d.shape
    const = lambda i: (0, 0)
    single = pl.Buffered(1)
    return pl.pallas_call(
        functools.partial(_mlp_kernel, ff_chunk=ff_chunk),
        grid=(N // tile,),
        in_specs=[
            pl.BlockSpec((tile, D), lambda i: (i, 0)),
            pl.BlockSpec((tile, mix_a.shape[1]), lambda i: (i, 0)),
            pl.BlockSpec((tile, mix_b.shape[1]), lambda i: (i, 0)),
            pl.BlockSpec(w_out.shape, const, pipeline_mode=single),
            pl.BlockSpec(g1.shape, const),
            pl.BlockSpec(w1.shape, const, pipeline_mode=single),
            pl.BlockSpec(w2.shape, const, pipeline_mode=single),
            pl.BlockSpec(g2.shape, const),
        ],
        out_specs=pl.BlockSpec((tile, D), lambda i: (i, 0)),
        out_shape=jax.ShapeDtypeStruct((N, D), jnp.float32),
        compiler_params=pltpu.CompilerParams(
            dimension_semantics=("arbitrary",), vmem_limit_bytes=VMEM_LIMIT),
        name="mlp",
    )(x2d, mix_a, mix_b, w_out, g1, w1, w2, g2)


def _swap_halves(w):
    half = w.shape[-1] // 2
    return jnp.concatenate([w[..., half:], w[..., :half]], axis=-1)


def kernel(x, positions, rel_bias, norm_attn, w_in, diff_lq1, diff_lk1, diff_lq2, diff_lk2,
           diff_subln, mla_q_norm, mla_w_uq, mla_kv_norm, mla_w_ukv, w_out, norm_mlp,
           w_mlp_in, w_mlp_out, norm_final):
    B, S, D = x.shape
    bf = jnp.bfloat16
    depth = w_in.shape[0]
    assert depth == 1
    l = 0
    row = lambda a: a.reshape(1, -1).astype(jnp.float32)

    w_in_l = w_in[l]
    k_pe_cols = w_in_l[:, -MLA_ROPE_DIM:]
    w_in_x = jnp.concatenate([w_in_l, _swap_halves(k_pe_cols)], axis=1).astype(bf)
    w_uq = mla_w_uq[l].reshape(MLA_Q_RANK, MLA_HEADS, MLA_NOPE_DIM + MLA_ROPE_DIM)
    q_pe_cols = w_uq[..., MLA_NOPE_DIM:]
    w_uq_x = jnp.concatenate([w_uq, _swap_halves(q_pe_cols)], axis=-1)
    w_uq_x = w_uq_x.reshape(MLA_Q_RANK, MLA_HEADS * MLA_QK_PAD).astype(bf)
    w_ukv = mla_w_ukv[l].astype(bf)

    tab = _rope_table(positions, 512)
    dq, dk, dv, mq, mk, mv = _proj(x, row(norm_attn[l]), w_in_x, row(mla_q_norm[l]), w_uq_x,
                                   row(mla_kv_norm[l]), w_ukv, tab, 512)
    mix_a = _diff_attn(dq, dk, dv, positions, rel_bias, diff_lq1[l], diff_lk1[l],
                       diff_lq2[l], diff_lk2[l], diff_subln[l], 256, 256)
    mix_b = _mla_attn(mq, mk, mv, positions, 512, 256)
    out = _mlp(x.reshape(B * S, D), mix_a.reshape(B * S, -1), mix_b.reshape(B * S, -1),
               w_out[l].astype(bf), row(norm_mlp[l]), w_mlp_in[l].astype(bf),
               w_mlp_out[l].astype(bf), row(norm_final), 512, 512)
    return out.reshape(B, S, D)
```

```python
import functools
import math

import jax
import jax.numpy as jnp
from jax import lax
from jax.experimental import pallas as pl
from jax.experimental.pallas import tpu as pltpu

D_MODEL = 1024
DIFF_HEADS = 4
DIFF_HEAD_DIM = 64
DIFF_V_DIM = 128
DIFF_QK_COLS = DIFF_HEADS * 2 * DIFF_HEAD_DIM
DIFF_V_COLS = DIFF_HEADS * DIFF_V_DIM
MLA_HEADS = 4
MLA_Q_RANK = 384
MLA_KV_RANK = 256
MLA_NOPE_DIM = 128
MLA_ROPE_DIM = 64
MLA_V_DIM = 128
MLA_QK_PAD = 256
ROPE_BASE = 10000.0
D_FF = 4 * D_MODEL
REL_BUCKETS = 32
REL_MAX_DIST = 128
NORM_EPS = 1e-6
NEG_INF = -1e30
LOG2E = math.log2(math.e)
LAMBDA_INIT = 0.8 - 0.6 * math.exp(-0.3 * 0)

LANES = 128
VMEM_LIMIT = 48 * 1024 * 1024

NT_DIMS = (((1,), (1,)), ((), ()))


def _rms(x, gain):
    return x * lax.rsqrt(jnp.mean(x * x, axis=-1, keepdims=True) + NORM_EPS) * gain


def _rope_table_kernel(pos_ref, tab_ref):
    half = MLA_ROPE_DIM // 2
    lane = lax.broadcasted_iota(jnp.int32, (1, LANES), 1)
    fidx = (lane % half).astype(jnp.float32) * 2.0
    inv_freq = jnp.exp(-(fidx / MLA_ROPE_DIM) * math.log(ROPE_BASE))
    ang = pos_ref[...].astype(jnp.float32) * inv_freq
    c, s = jnp.cos(ang), jnp.sin(ang)
    tab_ref[...] = jnp.where(lane < 2 * half, c, jnp.where(lane < 3 * half, -s, s))


def _rope_table(positions, tile):
    seq = positions.shape[0]
    return pl.pallas_call(
        _rope_table_kernel,
        grid=(seq // tile,),
        in_specs=[pl.BlockSpec((tile, 1), lambda i: (i, 0))],
        out_specs=pl.BlockSpec((tile, LANES), lambda i: (i, 0)),
        out_shape=jax.ShapeDtypeStruct((seq, LANES), jnp.float32),
        name="rope_table",
    )(positions.reshape(seq, 1))


def _rope_slab(slab, tab):
    prod = slab * tab
    return prod + pltpu.roll(prod, MLA_ROPE_DIM, axis=1)


def _proj_kernel(x_ref, g_ref, w_in_ref, qn_ref, w_uq_ref, kvn_ref, w_ukv_ref, tab_ref,
                 dq_ref, dk_ref, dv_ref, mq_ref, mk_ref, mv_ref):
    bf = jnp.bfloat16
    h = _rms(x_ref[0], g_ref[...]).astype(bf)
    p = jnp.dot(h, w_in_ref[...], preferred_element_type=jnp.float32)
    tab = tab_ref[...]
    lane = lax.broadcasted_iota(jnp.int32, tab.shape, 1)

    dscale = (DIFF_HEAD_DIM ** -0.5) * LOG2E
    o_k, o_v = DIFF_QK_COLS, 2 * DIFF_QK_COLS
    for hh in range(DIFF_HEADS):
        sl = slice(hh * LANES, (hh + 1) * LANES)
        dq_ref[0, hh] = (p[:, sl] * dscale).astype(bf)
        dk_ref[0, hh] = p[:, o_k + hh * LANES:o_k + (hh + 1) * LANES].astype(bf)
        dv_ref[0, hh] = p[:, o_v + hh * LANES:o_v + (hh + 1) * LANES].astype(bf)

    o_cq = 2 * DIFF_QK_COLS + DIFF_V_COLS
    o_ckv = o_cq + MLA_Q_RANK
    o_kpe = o_ckv + MLA_KV_RANK
    cq = _rms(p[:, o_cq:o_ckv], qn_ref[...]).astype(bf)
    q = jnp.dot(cq, w_uq_ref[...], preferred_element_type=jnp.float32)
    ckv = _rms(p[:, o_ckv:o_kpe], kvn_ref[...]).astype(bf)
    kv = jnp.dot(ckv, w_ukv_ref[...], preferred_element_type=jnp.float32)
    k_rope = _rope_slab(p[:, o_kpe:o_kpe + LANES], tab).astype(bf)

    mscale = ((MLA_NOPE_DIM + MLA_ROPE_DIM) ** -0.5) * LOG2E
    for hh in range(MLA_HEADS):
        base = hh * MLA_QK_PAD
        q_rope = _rope_slab(q[:, base + LANES:base + 2 * LANES], tab)
        q_rope = jnp.where(lane < MLA_ROPE_DIM, q_rope, 0.0)
        mq_ref[0, hh, :, 0:LANES] = (q[:, base:base + LANES] * mscale).astype(bf)
        mq_ref[0, hh, :, LANES:2 * LANES] = (q_rope * mscale).astype(bf)
        mk_ref[0, hh, :, 0:LANES] = kv[:, base:base + LANES].astype(bf)
        mk_ref[0, hh, :, LANES:2 * LANES] = k_rope
        mv_ref[0, hh] = kv[:, base + LANES:base + 2 * LANES].astype(bf)


def _proj(x, g, w_in, qn, w_uq, kvn, w_ukv, tab, tile):
    B, S, D = x.shape
    bf = jnp.bfloat16
    const = lambda b, i: (0, 0)
    head_out = lambda w: pl.BlockSpec((1, DIFF_HEADS, tile, w), lambda b, i: (b, 0, i, 0))
    shp = lambda w: jax.ShapeDtypeStruct((B, DIFF_HEADS, S, w), bf)
    return pl.pallas_call(
        _proj_kernel,
        grid=(B, S // tile),
        in_specs=[
            pl.BlockSpec((1, tile, D), lambda b, i: (b, i, 0)),
            pl.BlockSpec(g.shape, const),
            pl.BlockSpec(w_in.shape, const),
            pl.BlockSpec(qn.shape, const),
            pl.BlockSpec(w_uq.shape, const),
            pl.BlockSpec(kvn.shape, const),
            pl.BlockSpec(w_ukv.shape, const),
            pl.BlockSpec((tile, LANES), lambda b, i: (i, 0)),
        ],
        out_specs=[head_out(LANES), head_out(LANES), head_out(LANES),
                   head_out(MLA_QK_PAD), head_out(MLA_QK_PAD), head_out(LANES)],
        out_shape=[shp(LANES), shp(LANES), shp(LANES),
                   shp(MLA_QK_PAD), shp(MLA_QK_PAD), shp(LANES)],
        compiler_params=pltpu.CompilerParams(
            dimension_semantics=("arbitrary", "arbitrary"), vmem_limit_bytes=VMEM_LIMIT),
        name="proj",
    )(x, g, w_in, qn, w_uq, kvn, w_ukv, tab)


def _schedule(positions, tq, tk, far):
    seq = positions.shape[0]
    qp = positions.reshape(seq // tq, tq)
    kp = positions.reshape(seq // tk, tk)
    qmin, qmax = qp.min(axis=1), qp.max(axis=1)
    kmin, kmax = kp.min(axis=1), kp.max(axis=1)
    skip = kmin[None, :] > qmax[:, None]
    interior = (qmin[:, None] - kmax[None, :]) >= far
    cls = jnp.where(interior, 0, jnp.where(skip, 2, 1)).astype(jnp.int32)
    order = jnp.argsort(cls, axis=1, stable=True).astype(jnp.int32)
    n_int = jnp.sum(cls == 0, axis=1).astype(jnp.int32)
    n_vis = jnp.sum(cls < 2, axis=1).astype(jnp.int32)
    return order.reshape(-1), n_int, n_vis


def _flash_step(qq, k, v, add, m_ref, acc_ref):
    tk = k.shape[0]
    s = lax.dot_general(qq, k, NT_DIMS, preferred_element_type=jnp.float32) + add
    m_old = m_ref[...]
    m_new = jnp.maximum(m_old, jnp.max(s, axis=1, keepdims=True))
    alpha = jnp.exp2(m_old - m_new)
    p = jnp.concatenate([jnp.exp2(s[:, c * LANES:(c + 1) * LANES] - m_new)
                         for c in range(tk // LANES)], axis=1).astype(jnp.bfloat16)
    v1 = jnp.concatenate([v, jnp.ones_like(v)], axis=1)
    pv = jnp.dot(p, v1, preferred_element_type=jnp.float32)
    acc_ref[...] = jnp.concatenate([alpha, alpha], axis=1) * acc_ref[...] + pv
    m_ref[...] = m_new


def _flash_init(m_ref, acc_ref):
    m_ref[...] = jnp.full(m_ref.shape, NEG_INF, jnp.float32)
    acc_ref[...] = jnp.zeros(acc_ref.shape, jnp.float32)


def _flash_out(acc_ref, hh):
    acc = acc_ref[hh]
    return acc[:, :LANES] / acc[:, LANES:]


def _t5_bias_row(relb_ref, head):
    n = lax.broadcasted_iota(jnp.int32, (1, LANES), 1)
    max_exact = REL_BUCKETS // 2
    nf = jnp.maximum(n, 1).astype(jnp.float32)
    large = max_exact + (jnp.log(nf / max_exact) / math.log(REL_MAX_DIST / max_exact)
                         * (REL_BUCKETS - max_exact)).astype(jnp.int32)
    large = jnp.minimum(large, REL_BUCKETS - 1)
    bucket = jnp.where(n < max_exact, n, large)
    row = jnp.zeros((1, LANES), jnp.float32)
    for b in range(REL_BUCKETS):
        row = jnp.where(bucket == b, relb_ref[b, head] * LOG2E, row)
    return row


def _diff_kernel(tbl_ref, nint_ref, nvis_ref, relb_ref,
                 q_ref, k_ref, v_ref, qpos_ref, kpos_ref,
                 lq1_ref, lk1_ref, lq2_ref, lk2_ref, subln_ref,
                 o_ref, m_ref, acc_ref, *, tq, tk, nk, hp):
    head0 = pl.program_id(1) * hp
    i = pl.program_id(2)
    lane = lax.broadcasted_iota(jnp.int32, (tq, LANES), 1)
    qqs = []
    for hh in range(hp):
        q = q_ref[0, hh]
        zero = jnp.zeros_like(q)
        qqs.append(jnp.concatenate([jnp.where(lane < DIFF_HEAD_DIM, q, zero),
                                    jnp.where(lane >= DIFF_HEAD_DIM, q, zero)], axis=0))
    _flash_init(m_ref, acc_ref)
    bias_rows = [_t5_bias_row(relb_ref, head0 + hh) for hh in range(hp)]
    far_bias = [relb_ref[REL_BUCKETS - 1, head0 + hh] * LOG2E for hh in range(hp)]

    def interior(t, carry):
        j = tbl_ref[i * nk + t]
        off = pl.multiple_of(j * tk, tk)
        for hh in range(hp):
            _flash_step(qqs[hh], k_ref[0, hh, pl.ds(off, tk), :], v_ref[0, hh, pl.ds(off, tk), :],
                        far_bias[hh], m_ref.at[hh], acc_ref.at[hh])
        return carry

    def general(t, carry):
        j = tbl_ref[i * nk + t]
        off = pl.multiple_of(j * tk, tk)
        qpos = qpos_ref[...]
        kpos = kpos_ref[j]
        ds = [qpos - kpos[:, c * LANES:(c + 1) * LANES] for c in range(tk // LANES)]
        for hh in range(hp):
            tab = jnp.broadcast_to(bias_rows[hh], (tq, LANES))
            add = jnp.concatenate(
                [jnp.where(d >= 0, jnp.take_along_axis(tab, jnp.clip(d, 0, LANES - 1), axis=1),
                           NEG_INF) for d in ds], axis=1)
            _flash_step(qqs[hh], k_ref[0, hh, pl.ds(off, tk), :], v_ref[0, hh, pl.ds(off, tk), :],
                        jnp.concatenate([add, add], axis=0), m_ref.at[hh], acc_ref.at[hh])
        return carry

    n_int = nint_ref[i]
    lax.fori_loop(0, n_int, interior, 0)
    lax.fori_loop(n_int, nvis_ref[i], general, 0)

    lam = (jnp.exp(jnp.sum(lq1_ref[...] * lk1_ref[...], axis=1, keepdims=True))
           - jnp.exp(jnp.sum(lq2_ref[...] * lk2_ref[...], axis=1, keepdims=True))
           + LAMBDA_INIT)
    for hh in range(hp):
        o = _flash_out(acc_ref, hh)
        oo = o[:tq] - lam * o[tq:]
        o_ref[0, :, hh * LANES:(hh + 1) * LANES] = (
            _rms(oo, subln_ref[...]) * (1.0 - LAMBDA_INIT)).astype(o_ref.dtype)


def _lane_rep(positions):
    return jnp.broadcast_to(positions[:, None], (positions.shape[0], LANES))


def _diff_attn(dq, dk, dv, positions, rel_bias, lq1, lk1, lq2, lk2, subln, tq, tk, hp):
    B, H, S, _ = dq.shape
    nq, nk = S // tq, S // tk
    tbl, n_int, n_vis = _schedule(positions, tq, tk, LANES)
    row = lambda a: a.reshape(1, -1).astype(jnp.float32)
    const = lambda b, h, i, *_: (0, 0)
    grid_spec = pltpu.PrefetchScalarGridSpec(
        num_scalar_prefetch=4,
        grid=(B, H // hp, nq),
        in_specs=[
            pl.BlockSpec((1, hp, tq, LANES), lambda b, h, i, *_: (b, h, i, 0)),
            pl.BlockSpec((1, hp, S, LANES), lambda b, h, i, *_: (b, h, 0, 0)),
            pl.BlockSpec((1, hp, S, LANES), lambda b, h, i, *_: (b, h, 0, 0)),
            pl.BlockSpec((tq, LANES), lambda b, h, i, *_: (i, 0)),
            pl.BlockSpec((nk, 1, tk), lambda b, h, i, *_: (0, 0, 0)),
            pl.BlockSpec((1, DIFF_HEAD_DIM), const),
            pl.BlockSpec((1, DIFF_HEAD_DIM), const),
            pl.BlockSpec((1, DIFF_HEAD_DIM), const),
            pl.BlockSpec((1, DIFF_HEAD_DIM), const),
            pl.BlockSpec((1, DIFF_V_DIM), const),
        ],
        out_specs=pl.BlockSpec((1, tq, hp * LANES), lambda b, h, i, *_: (b, i, h)),
        scratch_shapes=[pltpu.VMEM((hp, 2 * tq, LANES), jnp.float32),
                        pltpu.VMEM((hp, 2 * tq, 2 * LANES), jnp.float32)],
    )
    return pl.pallas_call(
        functools.partial(_diff_kernel, tq=tq, tk=tk, nk=nk, hp=hp),
        grid_spec=grid_spec,
        out_shape=jax.ShapeDtypeStruct((B, S, H * DIFF_V_DIM), jnp.bfloat16),
        compiler_params=pltpu.CompilerParams(
            dimension_semantics=("arbitrary",) * 3, vmem_limit_bytes=VMEM_LIMIT),
        name="diff_attn",
    )(tbl, n_int, n_vis, rel_bias.astype(jnp.float32),
      dq, dk, dv, _lane_rep(positions), positions.reshape(nk, 1, tk),
      row(lq1), row(lk1), row(lq2), row(lk2), row(subln))


def _mla_kernel(tbl_ref, nint_ref, nvis_ref,
                q_ref, k_ref, v_ref, qpos_ref, kpos_ref,
                o_ref, m_ref, acc_ref, *, tq, tk, nk, hp):
    i = pl.program_id(2)
    _flash_init(m_ref, acc_ref)

    def interior(t, carry):
        j = tbl_ref[i * nk + t]
        off = pl.multiple_of(j * tk, tk)
        for hh in range(hp):
            _flash_step(q_ref[0, hh], k_ref[0, hh, pl.ds(off, tk), :],
                        v_ref[0, hh, pl.ds(off, tk), :], 0.0, m_ref.at[hh], acc_ref.at[hh])
        return carry

    def general(t, carry):
        j = tbl_ref[i * nk + t]
        off = pl.multiple_of(j * tk, tk)
        qpos = qpos_ref[...]
        kpos = kpos_ref[j]
        add = jnp.concatenate(
            [jnp.where(qpos >= kpos[:, c * LANES:(c + 1) * LANES], 0.0, NEG_INF)
             for c in range(tk // LANES)], axis=1)
        for hh in range(hp):
            _flash_step(q_ref[0, hh], k_ref[0, hh, pl.ds(off, tk), :],
                        v_ref[0, hh, pl.ds(off, tk), :], add, m_ref.at[hh], acc_ref.at[hh])
        return carry

    n_int = nint_ref[i]
    lax.fori_loop(0, n_int, interior, 0)
    lax.fori_loop(n_int, nvis_ref[i], general, 0)
    for hh in range(hp):
        o_ref[0, :, hh * LANES:(hh + 1) * LANES] = _flash_out(acc_ref, hh).astype(o_ref.dtype)


def _mla_attn(mq, mk, mv, positions, tq, tk, hp):
    B, H, S, _ = mq.shape
    nq, nk = S // tq, S // tk
    tbl, n_int, n_vis = _schedule(positions, tq, tk, 0)
    grid_spec = pltpu.PrefetchScalarGridSpec(
        num_scalar_prefetch=3,
        grid=(B, H // hp, nq),
        in_specs=[
            pl.BlockSpec((1, hp, tq, MLA_QK_PAD), lambda b, h, i, *_: (b, h, i, 0)),
            pl.BlockSpec((1, hp, S, MLA_QK_PAD), lambda b, h, i, *_: (b, h, 0, 0)),
            pl.BlockSpec((1, hp, S, MLA_V_DIM), lambda b, h, i, *_: (b, h, 0, 0)),
            pl.BlockSpec((tq, LANES), lambda b, h, i, *_: (i, 0)),
            pl.BlockSpec((nk, 1, tk), lambda b, h, i, *_: (0, 0, 0)),
        ],
        out_specs=pl.BlockSpec((1, tq, hp * LANES), lambda b, h, i, *_: (b, i, h)),
        scratch_shapes=[pltpu.VMEM((hp, tq, LANES), jnp.float32),
                        pltpu.VMEM((hp, tq, 2 * LANES), jnp.float32)],
    )
    return pl.pallas_call(
        functools.partial(_mla_kernel, tq=tq, tk=tk, nk=nk, hp=hp),
        grid_spec=grid_spec,
        out_shape=jax.ShapeDtypeStruct((B, S, H * MLA_V_DIM), jnp.bfloat16),
        compiler_params=pltpu.CompilerParams(
            dimension_semantics=("arbitrary",) * 3, vmem_limit_bytes=VMEM_LIMIT),
        name="mla_attn",
    )(tbl, n_int, n_vis, mq, mk, mv, _lane_rep(positions), positions.reshape(nk, 1, tk))


def _mlp_kernel(x_ref, ma_ref, mb_ref, wo_ref, g1_ref, w1_ref, w2_ref, g2_ref, o_ref, *, ff_chunk):
    bf = jnp.bfloat16
    n_a = ma_ref.shape[1]
    x1 = (x_ref[...]
          + jnp.dot(ma_ref[...], wo_ref[0:n_a, :], preferred_element_type=jnp.float32)
          + jnp.dot(mb_ref[...], wo_ref[n_a:, :], preferred_element_type=jnp.float32))
    h = _rms(x1, g1_ref[...]).astype(bf)
    y = jnp.zeros_like(x1)
    for c in range(D_FF // ff_chunk):
        sl = slice(c * ff_chunk, (c + 1) * ff_chunk)
        a = jnp.dot(h, w1_ref[:, sl], preferred_element_type=jnp.float32)
        a = jnp.square(jnp.maximum(a, 0.0)).astype(bf)
        y = y + jnp.dot(a, w2_ref[sl, :], preferred_element_type=jnp.float32)
    o_ref[...] = _rms(x1 + y, g2_ref[...])


def _mlp(x2d, mix_a, mix_b, w_out, g1, w1, w2, g2, tile, ff_chunk):
    N, D = x2d.shape
    const = lambda i: (0, 0)
    single = pl.Buffered(1)
    return pl.pallas_call(
        functools.partial(_mlp_kernel, ff_chunk=ff_chunk),
        grid=(N // tile,),
        in_specs=[
            pl.BlockSpec((tile, D), lambda i: (i, 0)),
            pl.BlockSpec((tile, mix_a.shape[1]), lambda i: (i, 0)),
            pl.BlockSpec((tile, mix_b.shape[1]), lambda i: (i, 0)),
            pl.BlockSpec(w_out.shape, const, pipeline_mode=single),
            pl.BlockSpec(g1.shape, const),
            pl.BlockSpec(w1.shape, const, pipeline_mode=single),
            pl.BlockSpec(w2.shape, const, pipeline_mode=single),
            pl.BlockSpec(g2.shape, const),
        ],
        out_specs=pl.BlockSpec((tile, D), lambda i: (i, 0)),
        out_shape=jax.ShapeDtypeStruct((N, D), jnp.float32),
        compiler_params=pltpu.CompilerParams(
            dimension_semantics=("arbitrary",), vmem_limit_bytes=VMEM_LIMIT),
        name="mlp",
    )(x2d, mix_a, mix_b, w_out, g1, w1, w2, g2)


def _swap_halves(w):
    half = w.shape[-1] // 2
    return jnp.concatenate([w[..., half:], w[..., :half]], axis=-1)


def kernel(x, positions, rel_bias, norm_attn, w_in, diff_lq1, diff_lk1, diff_lq2, diff_lk2,
           diff_subln, mla_q_norm, mla_w_uq, mla_kv_norm, mla_w_ukv, w_out, norm_mlp,
           w_mlp_in, w_mlp_out, norm_final):
    B, S, D = x.shape
    bf = jnp.bfloat16
    depth = w_in.shape[0]
    assert depth == 1
    l = 0
    row = lambda a: a.reshape(1, -1).astype(jnp.float32)

    w_in_l = w_in[l]
    k_pe_cols = w_in_l[:, -MLA_ROPE_DIM:]
    w_in_x = jnp.concatenate([w_in_l, _swap_halves(k_pe_cols)], axis=1).astype(bf)
    w_uq = mla_w_uq[l].reshape(MLA_Q_RANK, MLA_HEADS, MLA_NOPE_DIM + MLA_ROPE_DIM)
    q_pe_cols = w_uq[..., MLA_NOPE_DIM:]
    w_uq_x = jnp.concatenate([w_uq, _swap_halves(q_pe_cols)], axis=-1)
    w_uq_x = w_uq_x.reshape(MLA_Q_RANK, MLA_HEADS * MLA_QK_PAD).astype(bf)
    w_ukv = mla_w_ukv[l].astype(bf)

    tab = _rope_table(positions, 512)
    dq, dk, dv, mq, mk, mv = _proj(x, row(norm_attn[l]), w_in_x, row(mla_q_norm[l]), w_uq_x,
                                   row(mla_kv_norm[l]), w_ukv, tab, 512)
    mix_a = _diff_attn(dq, dk, dv, positions, rel_bias, diff_lq1[l], diff_lk1[l],
                       diff_lq2[l], diff_lk2[l], diff_subln[l], 256, 256, 2)
    mix_b = _mla_attn(mq, mk, mv, positions, 512, 256, 2)
    out = _mlp(x.reshape(B * S, D), mix_a.reshape(B * S, -1), mix_b.reshape(B * S, -1),
               w_out[l].astype(bf), row(norm_mlp[l]), w_mlp_in[l].astype(bf),
               w_mlp_out[l].astype(bf), row(norm_final), 512, 512)
    return out.reshape(B, S, D)
```

```python
import functools
import math

import jax
import jax.numpy as jnp
from jax import lax
from jax.experimental import pallas as pl
from jax.experimental.pallas import tpu as pltpu

D_MODEL = 1024
DIFF_HEADS = 4
DIFF_HEAD_DIM = 64
DIFF_V_DIM = 128
DIFF_QK_COLS = DIFF_HEADS * 2 * DIFF_HEAD_DIM
DIFF_V_COLS = DIFF_HEADS * DIFF_V_DIM
MLA_HEADS = 4
MLA_Q_RANK = 384
MLA_KV_RANK = 256
MLA_NOPE_DIM = 128
MLA_ROPE_DIM = 64
MLA_V_DIM = 128
MLA_QK_PAD = 256
ROPE_BASE = 10000.0
D_FF = 4 * D_MODEL
REL_BUCKETS = 32
REL_MAX_DIST = 128
NORM_EPS = 1e-6
NEG_INF = -1e30
LOG2E = math.log2(math.e)
LAMBDA_INIT = 0.8 - 0.6 * math.exp(-0.3 * 0)

LANES = 128
VMEM_LIMIT = 48 * 1024 * 1024

NT_DIMS = (((1,), (1,)), ((), ()))


def _rms(x, gain):
    return x * lax.rsqrt(jnp.mean(x * x, axis=-1, keepdims=True) + NORM_EPS) * gain


def _rope_table_kernel(pos_ref, tab_ref):
    half = MLA_ROPE_DIM // 2
    lane = lax.broadcasted_iota(jnp.int32, (1, LANES), 1)
    fidx = (lane % half).astype(jnp.float32) * 2.0
    inv_freq = jnp.exp(-(fidx / MLA_ROPE_DIM) * math.log(ROPE_BASE))
    ang = pos_ref[...].astype(jnp.float32) * inv_freq
    c, s = jnp.cos(ang), jnp.sin(ang)
    tab_ref[...] = jnp.where(lane < 2 * half, c, jnp.where(lane < 3 * half, -s, s))


def _rope_table(positions, tile):
    seq = positions.shape[0]
    return pl.pallas_call(
        _rope_table_kernel,
        grid=(seq // tile,),
        in_specs=[pl.BlockSpec((tile, 1), lambda i: (i, 0))],
        out_specs=pl.BlockSpec((tile, LANES), lambda i: (i, 0)),
        out_shape=jax.ShapeDtypeStruct((seq, LANES), jnp.float32),
        name="rope_table",
    )(positions.reshape(seq, 1))


def _rope_slab(slab, tab):
    prod = slab * tab
    return prod + pltpu.roll(prod, MLA_ROPE_DIM, axis=1)


def _proj_kernel(x_ref, g_ref, w_in_ref, qn_ref, w_uq_ref, kvn_ref, w_ukv_ref, tab_ref,
                 dq_ref, dk_ref, dv_ref, mq_ref, mk_ref, mv_ref):
    bf = jnp.bfloat16
    h = _rms(x_ref[0], g_ref[...]).astype(bf)
    p = jnp.dot(h, w_in_ref[...], preferred_element_type=jnp.float32)
    tab = tab_ref[...]
    lane = lax.broadcasted_iota(jnp.int32, tab.shape, 1)

    dscale = (DIFF_HEAD_DIM ** -0.5) * LOG2E
    o_k, o_v = DIFF_QK_COLS, 2 * DIFF_QK_COLS
    for hh in range(DIFF_HEADS):
        sl = slice(hh * LANES, (hh + 1) * LANES)
        dq_ref[0, hh] = (p[:, sl] * dscale).astype(bf)
        dk_ref[0, hh] = p[:, o_k + hh * LANES:o_k + (hh + 1) * LANES].astype(bf)
        dv_ref[0, hh] = p[:, o_v + hh * LANES:o_v + (hh + 1) * LANES].astype(bf)

    o_cq = 2 * DIFF_QK_COLS + DIFF_V_COLS
    o_ckv = o_cq + MLA_Q_RANK
    o_kpe = o_ckv + MLA_KV_RANK
    cq = _rms(p[:, o_cq:o_ckv], qn_ref[...]).astype(bf)
    q = jnp.dot(cq, w_uq_ref[...], preferred_element_type=jnp.float32)
    ckv = _rms(p[:, o_ckv:o_kpe], kvn_ref[...]).astype(bf)
    kv = jnp.dot(ckv, w_ukv_ref[...], preferred_element_type=jnp.float32)
    k_rope = _rope_slab(p[:, o_kpe:o_kpe + LANES], tab).astype(bf)

    mscale = ((MLA_NOPE_DIM + MLA_ROPE_DIM) ** -0.5) * LOG2E
    for hh in range(MLA_HEADS):
        base = hh * MLA_QK_PAD
        q_rope = _rope_slab(q[:, base + LANES:base + 2 * LANES], tab)
        q_rope = jnp.where(lane < MLA_ROPE_DIM, q_rope, 0.0)
        mq_ref[0, hh, :, 0:LANES] = (q[:, base:base + LANES] * mscale).astype(bf)
        mq_ref[0, hh, :, LANES:2 * LANES] = (q_rope * mscale).astype(bf)
        mk_ref[0, hh, :, 0:LANES] = kv[:, base:base + LANES].astype(bf)
        mk_ref[0, hh, :, LANES:2 * LANES] = k_rope
        mv_ref[0, hh] = kv[:, base + LANES:base + 2 * LANES].astype(bf)


def _proj(x, g, w_in, qn, w_uq, kvn, w_ukv, tab, tile):
    B, S, D = x.shape
    bf = jnp.bfloat16
    const = lambda b, i: (0, 0)
    head_out = lambda w: pl.BlockSpec((1, DIFF_HEADS, tile, w), lambda b, i: (b, 0, i, 0))
    shp = lambda w: jax.ShapeDtypeStruct((B, DIFF_HEADS, S, w), bf)
    return pl.pallas_call(
        _proj_kernel,
        grid=(B, S // tile),
        in_specs=[
            pl.BlockSpec((1, tile, D), lambda b, i: (b, i, 0)),
            pl.BlockSpec(g.shape, const),
            pl.BlockSpec(w_in.shape, const),
            pl.BlockSpec(qn.shape, const),
            pl.BlockSpec(w_uq.shape, const),
            pl.BlockSpec(kvn.shape, const),
            pl.BlockSpec(w_ukv.shape, const),
            pl.BlockSpec((tile, LANES), lambda b, i: (i, 0)),
        ],
        out_specs=[head_out(LANES), head_out(LANES), head_out(LANES),
                   head_out(MLA_QK_PAD), head_out(MLA_QK_PAD), head_out(LANES)],
        out_shape=[shp(LANES), shp(LANES), shp(LANES),
                   shp(MLA_QK_PAD), shp(MLA_QK_PAD), shp(LANES)],
        compiler_params=pltpu.CompilerParams(
            dimension_semantics=("arbitrary", "arbitrary"), vmem_limit_bytes=VMEM_LIMIT),
        name="proj",
    )(x, g, w_in, qn, w_uq, kvn, w_ukv, tab)


def _schedule(positions, tq, tk, far):
    seq = positions.shape[0]
    qp = positions.reshape(seq // tq, tq)
    kp = positions.reshape(seq // tk, tk)
    qmin, qmax = qp.min(axis=1), qp.max(axis=1)
    kmin, kmax = kp.min(axis=1), kp.max(axis=1)
    skip = kmin[None, :] > qmax[:, None]
    interior = (qmin[:, None] - kmax[None, :]) >= far
    cls = jnp.where(interior, 0, jnp.where(skip, 2, 1)).astype(jnp.int32)
    order = jnp.argsort(cls, axis=1, stable=True).astype(jnp.int32)
    n_int = jnp.sum(cls == 0, axis=1).astype(jnp.int32)
    n_vis = jnp.sum(cls < 2, axis=1).astype(jnp.int32)
    return order.reshape(-1), n_int, n_vis


def _t5_bias_row(relb_ref, head):
    n = lax.broadcasted_iota(jnp.int32, (1, LANES), 1)
    max_exact = REL_BUCKETS // 2
    nf = jnp.maximum(n, 1).astype(jnp.float32)
    large = max_exact + (jnp.log(nf / max_exact) / math.log(REL_MAX_DIST / max_exact)
                         * (REL_BUCKETS - max_exact)).astype(jnp.int32)
    large = jnp.minimum(large, REL_BUCKETS - 1)
    bucket = jnp.where(n < max_exact, n, large)
    row = jnp.zeros((1, LANES), jnp.float32)
    for b in range(REL_BUCKETS):
        row = jnp.where(bucket == b, relb_ref[b, head] * LOG2E, row)
    return row


def _attn_kernel(*refs, tq, tk, nk, hp, diff):
    if diff:
        (tbl_ref, nint_ref, nvis_ref, relb_ref, q_ref, k_ref, v_ref, qpos_ref, kpos_ref,
         lq1_ref, lk1_ref, lq2_ref, lk2_ref, subln_ref,
         o_ref, m_ref, acc_ref, s0_ref, s1_ref, add_ref) = refs
    else:
        (tbl_ref, nint_ref, nvis_ref, q_ref, k_ref, v_ref, qpos_ref, kpos_ref,
         o_ref, m_ref, acc_ref, s0_ref, s1_ref, add_ref) = refs
    s_refs = (s0_ref, s1_ref)
    grp = pl.program_id(0)
    i = pl.program_id(1)
    n_vis = nvis_ref[i]
    n_int = jnp.minimum(nint_ref[i], n_vis - 1)
    nc = tk // LANES
    bf = jnp.bfloat16

    @pl.when(pl.program_id(2) == 0)
    def _build_add_tiles():
        qpos = qpos_ref[...]
        if diff:
            tabs = [jnp.broadcast_to(_t5_bias_row(relb_ref, grp * hp + hh), (tq, LANES))
                    for hh in range(hp)]

        def fill(slot, carry):
            kpos = kpos_ref[tbl_ref[i * nk + n_int + slot]]
            for c in range(nc):
                cs = slice(c * LANES, (c + 1) * LANES)
                d = qpos - kpos[:, cs]
                if diff:
                    idx = jnp.clip(d, 0, LANES - 1)
                    for hh in range(hp):
                        add_ref[slot, hh, :, cs] = jnp.where(
                            d >= 0, jnp.take_along_axis(tabs[hh], idx, axis=1), NEG_INF)
                else:
                    add_ref[slot, 0, :, cs] = jnp.where(d >= 0, 0.0, NEG_INF)
            return carry

        lax.fori_loop(0, n_vis - n_int, fill, 0)

    if diff:
        lane = lax.broadcasted_iota(jnp.int32, (tq, LANES), 1)
        qs = []
        for hh in range(hp):
            q = q_ref[0, hh]
            zero = jnp.zeros_like(q)
            qs.append(jnp.concatenate([jnp.where(lane < DIFF_HEAD_DIM, q, zero),
                                       jnp.where(lane >= DIFF_HEAD_DIM, q, zero)], axis=0))
        far_bias = [relb_ref[REL_BUCKETS - 1, grp * hp + hh] * LOG2E for hh in range(hp)]
    else:
        qs = [q_ref[0, hh] for hh in range(hp)]

    m_ref[...] = jnp.full(m_ref.shape, NEG_INF, jnp.float32)
    acc_ref[...] = jnp.zeros(acc_ref.shape, jnp.float32)

    def k_off(t):
        return pl.multiple_of(tbl_ref[i * nk + t] * tk, tk)

    def scores(t, s_ref):
        off = k_off(t)
        for hh in range(hp):
            s_ref[hh] = lax.dot_general(qs[hh], k_ref[0, hh, pl.ds(off, tk), :], NT_DIMS,
                                        preferred_element_type=jnp.float32)

    def softmax_pv(t, s_ref, interior):
        off = k_off(t)
        for hh in range(hp):
            s = s_ref[hh]
            m_old = m_ref[hh]
            if interior:
                m_cur = jnp.max(s, axis=1, keepdims=True)
                if diff:
                    m_new = jnp.maximum(m_old, m_cur + far_bias[hh])
                    shift = m_new - far_bias[hh]
                else:
                    m_new = jnp.maximum(m_old, m_cur)
                    shift = m_new
            else:
                add = add_ref[t - n_int, hh if diff else 0]
                s = s + (jnp.concatenate([add, add], axis=0) if diff else add)
                m_new = jnp.maximum(m_old, jnp.max(s, axis=1, keepdims=True))
                shift = m_new
            alpha = jnp.exp2(m_old - m_new)
            p = jnp.concatenate([jnp.exp2(s[:, c * LANES:(c + 1) * LANES] - shift)
                                 for c in range(nc)], axis=1).astype(bf)
            v = v_ref[0, hh, pl.ds(off, tk), :]
            pv = jnp.dot(p, jnp.concatenate([v, jnp.ones_like(v)], axis=1),
                         preferred_element_type=jnp.float32)
            acc_ref[hh] = jnp.concatenate([alpha, alpha], axis=1) * acc_ref[hh] + pv
            m_ref[hh] = m_new

    def step(interior):
        def body(t, carry):
            for par in (0, 1):
                @pl.when(t % 2 == par)
                def _():
                    scores(t + 1, s_refs[1 - par])
                    softmax_pv(t, s_refs[par], interior)
            return carry
        return body

    scores(0, s_refs[0])
    lax.fori_loop(0, n_int, step(True), 0)
    lax.fori_loop(n_int, n_vis - 1, step(False), 0)
    for par in (0, 1):
        @pl.when((n_vis - 1) % 2 == par)
        def _():
            softmax_pv(n_vis - 1, s_refs[par], False)

    if diff:
        lam = (jnp.exp(jnp.sum(lq1_ref[...] * lk1_ref[...], axis=1, keepdims=True))
               - jnp.exp(jnp.sum(lq2_ref[...] * lk2_ref[...], axis=1, keepdims=True))
               + LAMBDA_INIT)
    for hh in range(hp):
        acc = acc_ref[hh]
        o = acc[:, :LANES] / acc[:, LANES:]
        if diff:
            oo = o[:tq] - lam * o[tq:]
            o = _rms(oo, subln_ref[...]) * (1.0 - LAMBDA_INIT)
        o_ref[0, :, hh * LANES:(hh + 1) * LANES] = o.astype(o_ref.dtype)


def _lane_rep(positions):
    return jnp.broadcast_to(positions[:, None], (positions.shape[0], LANES))


def _attention(q, k, v, positions, tq, tk, hp, diff_params=None):
    B, H, S, width = q.shape
    diff = diff_params is not None
    nq, nk = S // tq, S // tk
    rows = 2 * tq if diff else tq
    tbl, n_int, n_vis = _schedule(positions, tq, tk, LANES if diff else 0)
    imap = lambda f: (lambda g, i, b, *_: f(g, i, b))
    const = imap(lambda g, i, b: (0, 0))
    in_specs = [
        pl.BlockSpec((1, hp, tq, width), imap(lambda g, i, b: (b, g, i, 0))),
        pl.BlockSpec((1, hp, S, width), imap(lambda g, i, b: (b, g, 0, 0))),
        pl.BlockSpec((1, hp, S, LANES), imap(lambda g, i, b: (b, g, 0, 0))),
        pl.BlockSpec((tq, LANES), imap(lambda g, i, b: (i, 0))),
        pl.BlockSpec((nk, 1, tk), imap(lambda g, i, b: (0, 0, 0))),
    ]
    prefetch = [tbl, n_int, n_vis]
    operands = [q, k, v, _lane_rep(positions), positions.reshape(nk, 1, tk)]
    if diff:
        rel_bias, lq1, lk1, lq2, lk2, subln = diff_params
        row = lambda a: a.reshape(1, -1).astype(jnp.float32)
        prefetch.append(rel_bias.astype(jnp.float32))
        operands += [row(lq1), row(lk1), row(lq2), row(lk2), row(subln)]
        in_specs += [pl.BlockSpec((1, DIFF_HEAD_DIM), const)] * 4 + [pl.BlockSpec((1, DIFF_V_DIM), const)]
    grid_spec = pltpu.PrefetchScalarGridSpec(
        num_scalar_prefetch=len(prefetch),
        grid=(H // hp, nq, B),
        in_specs=in_specs,
        out_specs=pl.BlockSpec((1, tq, hp * LANES), imap(lambda g, i, b: (b, i, g))),
        scratch_shapes=[pltpu.VMEM((hp, rows, LANES), jnp.float32),
                        pltpu.VMEM((hp, rows, 2 * LANES), jnp.float32),
                        pltpu.VMEM((hp, rows, tk), jnp.float32),
                        pltpu.VMEM((hp, rows, tk), jnp.float32),
                        pltpu.VMEM((nk, hp if diff else 1, tq, tk), jnp.float32)],
    )
    return pl.pallas_call(
        functools.partial(_attn_kernel, tq=tq, tk=tk, nk=nk, hp=hp, diff=diff),
        grid_spec=grid_spec,
        out_shape=jax.ShapeDtypeStruct((B, S, H * LANES), jnp.bfloat16),
        compiler_params=pltpu.CompilerParams(
            dimension_semantics=("arbitrary",) * 3, vmem_limit_bytes=VMEM_LIMIT),
        name="diff_attn" if diff else "mla_attn",
    )(*prefetch, *operands)


def _mlp_kernel(x_ref, ma_ref, mb_ref, wo_ref, g1_ref, w1_ref, w2_ref, g2_ref, o_ref, *, ff_chunk):
    bf = jnp.bfloat16
    n_a = ma_ref.shape[1]
    x1 = (x_ref[...]
          + jnp.dot(ma_ref[...], wo_ref[0:n_a, :], preferred_element_type=jnp.float32)
          + jnp.dot(mb_ref[...], wo_ref[n_a:, :], preferred_element_type=jnp.float32))
    h = _rms(x1, g1_ref[...]).astype(bf)
    y = jnp.zeros_like(x1)
    for c in range(D_FF // ff_chunk):
        sl = slice(c * ff_chunk, (c + 1) * ff_chunk)
        a = jnp.dot(h, w1_ref[:, sl], preferred_element_type=jnp.float32)
        a = jnp.square(jnp.maximum(a, 0.0)).astype(bf)
        y = y + jnp.dot(a, w2_ref[sl, :], preferred_element_type=jnp.float32)
    o_ref[...] = _rms(x1 + y, g2_ref[...])


def _mlp(x2d, mix_a, mix_b, w_out, g1, w1, w2, g2, tile, ff_chunk):
    N, D = x2d.shape
    const = lambda i: (0, 0)
    single = pl.Buffered(1)
    return pl.pallas_call(
        functools.partial(_mlp_kernel, ff_chunk=ff_chunk),
        grid=(N // tile,),
        in_specs=[
            pl.BlockSpec((tile, D), lambda i: (i, 0)),
            pl.BlockSpec((tile, mix_a.shape[1]), lambda i: (i, 0)),
            pl.BlockSpec((tile, mix_b.shape[1]), lambda i: (i, 0)),
            pl.BlockSpec(w_out.shape, const, pipeline_mode=single),
            pl.BlockSpec(g1.shape, const),
            pl.BlockSpec(w1.shape, const, pipeline_mode=single),
            pl.BlockSpec(w2.shape, const, pipeline_mode=single),
            pl.BlockSpec(g2.shape, const),
        ],
        out_specs=pl.BlockSpec((tile, D), lambda i: (i, 0)),
        out_shape=jax.ShapeDtypeStruct((N, D), jnp.float32),
        compiler_params=pltpu.CompilerParams(
            dimension_semantics=("arbitrary",), vmem_limit_bytes=VMEM_LIMIT),
        name="mlp",
    )(x2d, mix_a, mix_b, w_out, g1, w1, w2, g2)


def _swap_halves(w):
    half = w.shape[-1] // 2
    return jnp.concatenate([w[..., half:], w[..., :half]], axis=-1)


def kernel(x, positions, rel_bias, norm_attn, w_in, diff_lq1, diff_lk1, diff_lq2, diff_lk2,
           diff_subln, mla_q_norm, mla_w_uq, mla_kv_norm, mla_w_ukv, w_out, norm_mlp,
           w_mlp_in, w_mlp_out, norm_final):
    B, S, D = x.shape
    bf = jnp.bfloat16
    depth = w_in.shape[0]
    assert depth == 1
    l = 0
    row = lambda a: a.reshape(1, -1).astype(jnp.float32)

    w_in_l = w_in[l]
    k_pe_cols = w_in_l[:, -MLA_ROPE_DIM:]
    w_in_x = jnp.concatenate([w_in_l, _swap_halves(k_pe_cols)], axis=1).astype(bf)
    w_uq = mla_w_uq[l].reshape(MLA_Q_RANK, MLA_HEADS, MLA_NOPE_DIM + MLA_ROPE_DIM)
    q_pe_cols = w_uq[..., MLA_NOPE_DIM:]
    w_uq_x = jnp.concatenate([w_uq, _swap_halves(q_pe_cols)], axis=-1)
    w_uq_x = w_uq_x.reshape(MLA_Q_RANK, MLA_HEADS * MLA_QK_PAD).astype(bf)
    w_ukv = mla_w_ukv[l].astype(bf)

    tab = _rope_table(positions, 512)
    dq, dk, dv, mq, mk, mv = _proj(x, row(norm_attn[l]), w_in_x, row(mla_q_norm[l]), w_uq_x,
                                   row(mla_kv_norm[l]), w_ukv, tab, 512)
    mix_a = _attention(dq, dk, dv, positions, 256, 512, 2,
                       (rel_bias, diff_lq1[l], diff_lk1[l], diff_lq2[l], diff_lk2[l], diff_subln[l]))
    mix_b = _attention(mq, mk, mv, positions, 512, 512, 2)
    out = _mlp(x.reshape(B * S, D), mix_a.reshape(B * S, -1), mix_b.reshape(B * S, -1),
               w_out[l].astype(bf), row(norm_mlp[l]), w_mlp_in[l].astype(bf),
               w_mlp_out[l].astype(bf), row(norm_final), 512, 512)
    return out.reshape(B, S, D)
```

```python
import functools
import math

import jax
import jax.numpy as jnp
from jax import lax
from jax.experimental import pallas as pl
from jax.experimental.pallas import tpu as pltpu

D_MODEL = 1024
DIFF_HEADS = 4
DIFF_HEAD_DIM = 64
DIFF_V_DIM = 128
DIFF_QK_COLS = DIFF_HEADS * 2 * DIFF_HEAD_DIM
DIFF_V_COLS = DIFF_HEADS * DIFF_V_DIM
MLA_HEADS = 4
MLA_Q_RANK = 384
MLA_KV_RANK = 256
MLA_NOPE_DIM = 128
MLA_ROPE_DIM = 64
MLA_V_DIM = 128
MLA_QK_PAD = 256
ROPE_BASE = 10000.0
D_FF = 4 * D_MODEL
REL_BUCKETS = 32
REL_MAX_DIST = 128
NORM_EPS = 1e-6
NEG_INF = -1e30
LOG2E = math.log2(math.e)
LAMBDA_INIT = 0.8 - 0.6 * math.exp(-0.3 * 0)

LANES = 128
VMEM_LIMIT = 56 * 1024 * 1024

NT_DIMS = (((1,), (1,)), ((), ()))


def _rms(x, gain):
    return x * lax.rsqrt(jnp.mean(x * x, axis=-1, keepdims=True) + NORM_EPS) * gain


def _rope_table_kernel(pos_ref, tab_ref):
    half = MLA_ROPE_DIM // 2
    lane = lax.broadcasted_iota(jnp.int32, (1, LANES), 1)
    fidx = (lane % half).astype(jnp.float32) * 2.0
    inv_freq = jnp.exp(-(fidx / MLA_ROPE_DIM) * math.log(ROPE_BASE))
    ang = pos_ref[...].astype(jnp.float32) * inv_freq
    c, s = jnp.cos(ang), jnp.sin(ang)
    tab_ref[...] = jnp.where(lane < 2 * half, c, jnp.where(lane < 3 * half, -s, s))


def _rope_table(positions, tile):
    seq = positions.shape[0]
    return pl.pallas_call(
        _rope_table_kernel,
        grid=(seq // tile,),
        in_specs=[pl.BlockSpec((tile, 1), lambda i: (i, 0))],
        out_specs=pl.BlockSpec((tile, LANES), lambda i: (i, 0)),
        out_shape=jax.ShapeDtypeStruct((seq, LANES), jnp.float32),
        name="rope_table",
    )(positions.reshape(seq, 1))


def _rope_slab(slab, tab):
    prod = slab * tab
    return prod + pltpu.roll(prod, MLA_ROPE_DIM, axis=1)


def _proj_kernel(x_ref, g_ref, w_in_ref, qn_ref, w_uq_ref, kvn_ref, w_ukv_ref, tab_ref,
                 dq_ref, dk_ref, dv_ref, mq_ref, mk_ref, mv_ref):
    bf = jnp.bfloat16
    h = _rms(x_ref[0], g_ref[...]).astype(bf)
    p = jnp.dot(h, w_in_ref[...], preferred_element_type=jnp.float32)
    tab = tab_ref[...]
    lane = lax.broadcasted_iota(jnp.int32, tab.shape, 1)

    dscale = (DIFF_HEAD_DIM ** -0.5) * LOG2E
    o_k, o_v = DIFF_QK_COLS, 2 * DIFF_QK_COLS
    for hh in range(DIFF_HEADS):
        sl = slice(hh * LANES, (hh + 1) * LANES)
        dq_ref[0, hh] = (p[:, sl] * dscale).astype(bf)
        dk_ref[0, hh] = p[:, o_k + hh * LANES:o_k + (hh + 1) * LANES].astype(bf)
        dv_ref[0, hh] = p[:, o_v + hh * LANES:o_v + (hh + 1) * LANES].astype(bf)

    o_cq = 2 * DIFF_QK_COLS + DIFF_V_COLS
    o_ckv = o_cq + MLA_Q_RANK
    o_kpe = o_ckv + MLA_KV_RANK
    cq = _rms(p[:, o_cq:o_ckv], qn_ref[...]).astype(bf)
    q = jnp.dot(cq, w_uq_ref[...], preferred_element_type=jnp.float32)
    ckv = _rms(p[:, o_ckv:o_kpe], kvn_ref[...]).astype(bf)
    kv = jnp.dot(ckv, w_ukv_ref[...], preferred_element_type=jnp.float32)
    k_rope = _rope_slab(p[:, o_kpe:o_kpe + LANES], tab).astype(bf)

    mscale = ((MLA_NOPE_DIM + MLA_ROPE_DIM) ** -0.5) * LOG2E
    for hh in range(MLA_HEADS):
        base = hh * MLA_QK_PAD
        q_rope = _rope_slab(q[:, base + LANES:base + 2 * LANES], tab)
        q_rope = jnp.where(lane < MLA_ROPE_DIM, q_rope, 0.0)
        mq_ref[0, hh, :, 0:LANES] = (q[:, base:base + LANES] * mscale).astype(bf)
        mq_ref[0, hh, :, LANES:2 * LANES] = (q_rope * mscale).astype(bf)
        mk_ref[0, hh, :, 0:LANES] = kv[:, base:base + LANES].astype(bf)
        mk_ref[0, hh, :, LANES:2 * LANES] = k_rope
        mv_ref[0, hh] = kv[:, base + LANES:base + 2 * LANES].astype(bf)


def _proj(x, g, w_in, qn, w_uq, kvn, w_ukv, tab, tile):
    B, S, D = x.shape
    bf = jnp.bfloat16
    const = lambda b, i: (0, 0)
    head_out = lambda w: pl.BlockSpec((1, DIFF_HEADS, tile, w), lambda b, i: (b, 0, i, 0))
    shp = lambda w: jax.ShapeDtypeStruct((B, DIFF_HEADS, S, w), bf)
    return pl.pallas_call(
        _proj_kernel,
        grid=(B, S // tile),
        in_specs=[
            pl.BlockSpec((1, tile, D), lambda b, i: (b, i, 0)),
            pl.BlockSpec(g.shape, const),
            pl.BlockSpec(w_in.shape, const),
            pl.BlockSpec(qn.shape, const),
            pl.BlockSpec(w_uq.shape, const),
            pl.BlockSpec(kvn.shape, const),
            pl.BlockSpec(w_ukv.shape, const),
            pl.BlockSpec((tile, LANES), lambda b, i: (i, 0)),
        ],
        out_specs=[head_out(LANES), head_out(LANES), head_out(LANES),
                   head_out(MLA_QK_PAD), head_out(MLA_QK_PAD), head_out(LANES)],
        out_shape=[shp(LANES), shp(LANES), shp(LANES),
                   shp(MLA_QK_PAD), shp(MLA_QK_PAD), shp(LANES)],
        compiler_params=pltpu.CompilerParams(
            dimension_semantics=("arbitrary", "arbitrary"), vmem_limit_bytes=VMEM_LIMIT),
        name="proj",
    )(x, g, w_in, qn, w_uq, kvn, w_ukv, tab)


def _schedule(positions, tq, tk, far):
    seq = positions.shape[0]
    qp = positions.reshape(seq // tq, tq)
    kp = positions.reshape(seq // tk, tk)
    qmin, qmax = qp.min(axis=1), qp.max(axis=1)
    kmin, kmax = kp.min(axis=1), kp.max(axis=1)
    skip = kmin[None, :] > qmax[:, None]
    interior = (qmin[:, None] - kmax[None, :]) >= far
    cls = jnp.where(interior, 0, jnp.where(skip, 2, 1)).astype(jnp.int32)
    order = jnp.argsort(cls, axis=1, stable=True).astype(jnp.int32)
    n_int = jnp.sum(cls == 0, axis=1).astype(jnp.int32)
    n_vis = jnp.sum(cls < 2, axis=1).astype(jnp.int32)
    return order.reshape(-1), n_int, n_vis


def _t5_bias_row(relb_ref, head):
    n = lax.broadcasted_iota(jnp.int32, (1, LANES), 1)
    max_exact = REL_BUCKETS // 2
    nf = jnp.maximum(n, 1).astype(jnp.float32)
    large = max_exact + (jnp.log(nf / max_exact) / math.log(REL_MAX_DIST / max_exact)
                         * (REL_BUCKETS - max_exact)).astype(jnp.int32)
    large = jnp.minimum(large, REL_BUCKETS - 1)
    bucket = jnp.where(n < max_exact, n, large)
    row = jnp.zeros((1, LANES), jnp.float32)
    for b in range(REL_BUCKETS):
        row = jnp.where(bucket == b, relb_ref[b, head] * LOG2E, row)
    return row


def _attn_kernel(*refs, tq, tk, nk, hp, diff):
    if diff:
        (tbl_ref, nint_ref, nvis_ref, relb_ref, q_ref, k_ref, v_ref, qpos_ref, kpos_ref,
         lq1_ref, lk1_ref, lq2_ref, lk2_ref, subln_ref,
         o_ref, m_ref, acc_ref, s0_ref, s1_ref, add_ref) = refs
    else:
        (tbl_ref, nint_ref, nvis_ref, q_ref, k_ref, v_ref, qpos_ref, kpos_ref,
         o_ref, m_ref, acc_ref, s0_ref, s1_ref, add_ref) = refs
    s_refs = (s0_ref, s1_ref)
    grp = pl.program_id(0)
    i = pl.program_id(1)
    n_vis = nvis_ref[i]
    n_int = jnp.minimum(nint_ref[i], n_vis - 1)
    nc = tk // LANES
    bf = jnp.bfloat16

    @pl.when(pl.program_id(2) == 0)
    def _build_add_tiles():
        qpos = qpos_ref[...]
        if diff:
            tabs = [jnp.broadcast_to(_t5_bias_row(relb_ref, grp * hp + hh), (tq, LANES))
                    for hh in range(hp)]

        def fill(slot, carry):
            kpos = kpos_ref[tbl_ref[i * nk + n_int + slot]]
            for c in range(nc):
                cs = slice(c * LANES, (c + 1) * LANES)
                d = qpos - kpos[:, cs]
                if diff:
                    idx = jnp.clip(d, 0, LANES - 1)
                    for hh in range(hp):
                        add_ref[slot, hh, :, cs] = jnp.where(
                            d >= 0, jnp.take_along_axis(tabs[hh], idx, axis=1), NEG_INF)
                else:
                    add_ref[slot, 0, :, cs] = jnp.where(d >= 0, 0.0, NEG_INF)
            return carry

        lax.fori_loop(0, n_vis - n_int, fill, 0)

    if diff:
        lane = lax.broadcasted_iota(jnp.int32, (tq, LANES), 1)
        qs = []
        for hh in range(hp):
            q = q_ref[0, hh]
            zero = jnp.zeros_like(q)
            qs.append(jnp.concatenate([jnp.where(lane < DIFF_HEAD_DIM, q, zero),
                                       jnp.where(lane >= DIFF_HEAD_DIM, q, zero)], axis=0))
        far_bias = [relb_ref[REL_BUCKETS - 1, grp * hp + hh] * LOG2E for hh in range(hp)]
    else:
        qs = [q_ref[0, hh] for hh in range(hp)]

    m_ref[...] = jnp.full(m_ref.shape, NEG_INF, jnp.float32)
    acc_ref[...] = jnp.zeros(acc_ref.shape, jnp.float32)

    def k_off(t):
        return pl.multiple_of(tbl_ref[i * nk + t] * tk, tk)

    mxu_n = 2 * LANES
    n_sub = tk // mxu_n

    def softmax(t, s, hh, interior):
        m_old = m_ref[hh]
        if interior:
            m_cur = jnp.max(s, axis=1, keepdims=True)
            if diff:
                m_new = jnp.maximum(m_old, m_cur + far_bias[hh])
                shift = m_new - far_bias[hh]
            else:
                m_new = jnp.maximum(m_old, m_cur)
                shift = m_new
        else:
            add = add_ref[t - n_int, hh if diff else 0]
            s = s + (jnp.concatenate([add, add], axis=0) if diff else add)
            m_new = jnp.maximum(m_old, jnp.max(s, axis=1, keepdims=True))
            shift = m_new
        m_ref[hh] = m_new
        p = jnp.concatenate([jnp.exp2(s[:, c * LANES:(c + 1) * LANES] - shift)
                             for c in range(nc)], axis=1).astype(bf)
        return p, jnp.exp2(m_old - m_new)

    def visit(t_sm, s_cur, interior, t_sc, s_nxt):
        off_sm = None if t_sm is None else k_off(t_sm)
        off_sc = None if t_sc is None else k_off(t_sc)
        for hh in range(hp):
            if t_sm is not None:
                p, alpha = softmax(t_sm, s_cur[hh], hh, interior)
                pv = None
            for c in range(n_sub):
                cs = slice(c * mxu_n, (c + 1) * mxu_n)
                if t_sc is not None:
                    k = k_ref[0, hh, pl.ds(off_sc + c * mxu_n, mxu_n), :]
                    s_nxt[hh, :, cs] = lax.dot_general(qs[hh], k, NT_DIMS,
                                                       preferred_element_type=jnp.float32)
                if t_sm is not None:
                    v = v_ref[0, hh, pl.ds(off_sm + c * mxu_n, mxu_n), :]
                    d = jnp.dot(p[:, cs], jnp.concatenate([v, jnp.ones_like(v)], axis=1),
                                preferred_element_type=jnp.float32)
                    pv = d if pv is None else pv + d
            if t_sm is not None:
                acc_ref[hh] = jnp.concatenate([alpha, alpha], axis=1) * acc_ref[hh] + pv

    def step(interior):
        def body(t, carry):
            for par in (0, 1):
                @pl.when(t % 2 == par)
                def _():
                    visit(t, s_refs[par], interior, t + 1, s_refs[1 - par])
            return carry
        return body

    visit(None, None, False, 0, s_refs[0])
    lax.fori_loop(0, n_int, step(True), 0)
    lax.fori_loop(n_int, n_vis - 1, step(False), 0)
    for par in (0, 1):
        @pl.when((n_vis - 1) % 2 == par)
        def _():
            visit(n_vis - 1, s_refs[par], False, None, None)

    if diff:
        lam = (jnp.exp(jnp.sum(lq1_ref[...] * lk1_ref[...], axis=1, keepdims=True))
               - jnp.exp(jnp.sum(lq2_ref[...] * lk2_ref[...], axis=1, keepdims=True))
               + LAMBDA_INIT)
    for hh in range(hp):
        acc = acc_ref[hh]
        o = acc[:, :LANES] / acc[:, LANES:]
        if diff:
            oo = o[:tq] - lam * o[tq:]
            o = _rms(oo, subln_ref[...]) * (1.0 - LAMBDA_INIT)
        o_ref[0, :, hh * LANES:(hh + 1) * LANES] = o.astype(o_ref.dtype)


def _lane_rep(positions):
    return jnp.broadcast_to(positions[:, None], (positions.shape[0], LANES))


def _attention(q, k, v, positions, tq, tk, hp, diff_params=None):
    B, H, S, width = q.shape
    diff = diff_params is not None
    nq, nk = S // tq, S // tk
    rows = 2 * tq if diff else tq
    tbl, n_int, n_vis = _schedule(positions, tq, tk, LANES if diff else 0)
    imap = lambda f: (lambda g, i, b, *_: f(g, i, b))
    const = imap(lambda g, i, b: (0, 0))
    in_specs = [
        pl.BlockSpec((1, hp, tq, width), imap(lambda g, i, b: (b, g, i, 0))),
        pl.BlockSpec((1, hp, S, width), imap(lambda g, i, b: (b, g, 0, 0))),
        pl.BlockSpec((1, hp, S, LANES), imap(lambda g, i, b: (b, g, 0, 0))),
        pl.BlockSpec((tq, LANES), imap(lambda g, i, b: (i, 0))),
        pl.BlockSpec((nk, 1, tk), imap(lambda g, i, b: (0, 0, 0))),
    ]
    prefetch = [tbl, n_int, n_vis]
    operands = [q, k, v, _lane_rep(positions), positions.reshape(nk, 1, tk)]
    if diff:
        rel_bias, lq1, lk1, lq2, lk2, subln = diff_params
        row = lambda a: a.reshape(1, -1).astype(jnp.float32)
        prefetch.append(rel_bias.astype(jnp.float32))
        operands += [row(lq1), row(lk1), row(lq2), row(lk2), row(subln)]
        in_specs += [pl.BlockSpec((1, DIFF_HEAD_DIM), const)] * 4 + [pl.BlockSpec((1, DIFF_V_DIM), const)]
    grid_spec = pltpu.PrefetchScalarGridSpec(
        num_scalar_prefetch=len(prefetch),
        grid=(H // hp, nq, B),
        in_specs=in_specs,
        out_specs=pl.BlockSpec((1, tq, hp * LANES), imap(lambda g, i, b: (b, i, g))),
        scratch_shapes=[pltpu.VMEM((hp, rows, LANES), jnp.float32),
                        pltpu.VMEM((hp, rows, 2 * LANES), jnp.float32),
                        pltpu.VMEM((hp, rows, tk), jnp.float32),
                        pltpu.VMEM((hp, rows, tk), jnp.float32),
                        pltpu.VMEM((nk, hp if diff else 1, tq, tk), jnp.float32)],
    )
    return pl.pallas_call(
        functools.partial(_attn_kernel, tq=tq, tk=tk, nk=nk, hp=hp, diff=diff),
        grid_spec=grid_spec,
        out_shape=jax.ShapeDtypeStruct((B, S, H * LANES), jnp.bfloat16),
        compiler_params=pltpu.CompilerParams(
            dimension_semantics=("arbitrary",) * 3, vmem_limit_bytes=VMEM_LIMIT),
        name="diff_attn" if diff else "mla_attn",
    )(*prefetch, *operands)


def _mlp_kernel(x_ref, ma_ref, mb_ref, wo_ref, g1_ref, w1_ref, w2_ref, g2_ref, o_ref, *, ff_chunk):
    bf = jnp.bfloat16
    n_a = ma_ref.shape[1]
    x1 = (x_ref[...]
          + jnp.dot(ma_ref[...], wo_ref[0:n_a, :], preferred_element_type=jnp.float32)
          + jnp.dot(mb_ref[...], wo_ref[n_a:, :], preferred_element_type=jnp.float32))
    h = _rms(x1, g1_ref[...]).astype(bf)
    y = jnp.zeros_like(x1)
    for c in range(D_FF // ff_chunk):
        sl = slice(c * ff_chunk, (c + 1) * ff_chunk)
        a = jnp.dot(h, w1_ref[:, sl], preferred_element_type=jnp.float32)
        a = jnp.square(jnp.maximum(a, 0.0)).astype(bf)
        y = y + jnp.dot(a, w2_ref[sl, :], preferred_element_type=jnp.float32)
    o_ref[...] = _rms(x1 + y, g2_ref[...])


def _mlp(x2d, mix_a, mix_b, w_out, g1, w1, w2, g2, tile, ff_chunk):
    N, D = x2d.shape
    const = lambda i: (0, 0)
    single = pl.Buffered(1)
    return pl.pallas_call(
        functools.partial(_mlp_kernel, ff_chunk=ff_chunk),
        grid=(N // tile,),
        in_specs=[
            pl.BlockSpec((tile, D), lambda i: (i, 0)),
            pl.BlockSpec((tile, mix_a.shape[1]), lambda i: (i, 0)),
            pl.BlockSpec((tile, mix_b.shape[1]), lambda i: (i, 0)),
            pl.BlockSpec(w_out.shape, const, pipeline_mode=single),
            pl.BlockSpec(g1.shape, const),
            pl.BlockSpec(w1.shape, const, pipeline_mode=single),
            pl.BlockSpec(w2.shape, const, pipeline_mode=single),
            pl.BlockSpec(g2.shape, const),
        ],
        out_specs=pl.BlockSpec((tile, D), lambda i: (i, 0)),
        out_shape=jax.ShapeDtypeStruct((N, D), jnp.float32),
        compiler_params=pltpu.CompilerParams(
            dimension_semantics=("arbitrary",), vmem_limit_bytes=VMEM_LIMIT),
        name="mlp",
    )(x2d, mix_a, mix_b, w_out, g1, w1, w2, g2)


def _swap_halves(w):
    half = w.shape[-1] // 2
    return jnp.concatenate([w[..., half:], w[..., :half]], axis=-1)


def kernel(x, positions, rel_bias, norm_attn, w_in, diff_lq1, diff_lk1, diff_lq2, diff_lk2,
           diff_subln, mla_q_norm, mla_w_uq, mla_kv_norm, mla_w_ukv, w_out, norm_mlp,
           w_mlp_in, w_mlp_out, norm_final):
    B, S, D = x.shape
    bf = jnp.bfloat16
    depth = w_in.shape[0]
    assert depth == 1
    l = 0
    row = lambda a: a.reshape(1, -1).astype(jnp.float32)

    w_in_l = w_in[l]
    k_pe_cols = w_in_l[:, -MLA_ROPE_DIM:]
    w_in_x = jnp.concatenate([w_in_l, _swap_halves(k_pe_cols)], axis=1).astype(bf)
    w_uq = mla_w_uq[l].reshape(MLA_Q_RANK, MLA_HEADS, MLA_NOPE_DIM + MLA_ROPE_DIM)
    q_pe_cols = w_uq[..., MLA_NOPE_DIM:]
    w_uq_x = jnp.concatenate([w_uq, _swap_halves(q_pe_cols)], axis=-1)
    w_uq_x = w_uq_x.reshape(MLA_Q_RANK, MLA_HEADS * MLA_QK_PAD).astype(bf)
    w_ukv = mla_w_ukv[l].astype(bf)

    tab = _rope_table(positions, 512)
    dq, dk, dv, mq, mk, mv = _proj(x, row(norm_attn[l]), w_in_x, row(mla_q_norm[l]), w_uq_x,
                                   row(mla_kv_norm[l]), w_ukv, tab, 512)
    mix_a = _attention(dq, dk, dv, positions, 256, 512, 4,
                       (rel_bias, diff_lq1[l], diff_lk1[l], diff_lq2[l], diff_lk2[l], diff_subln[l]))
    mix_b = _attention(mq, mk, mv, positions, 512, 512, 4)
    out = _mlp(x.reshape(B * S, D), mix_a.reshape(B * S, -1), mix_b.reshape(B * S, -1),
               w_out[l].astype(bf), row(norm_mlp[l]), w_mlp_in[l].astype(bf),
               w_mlp_out[l].astype(bf), row(norm_final), 512, 512)
    return out.reshape(B, S, D)
```

```python
import functools
import math

import jax
import jax.numpy as jnp
from jax import lax
from jax.experimental import pallas as pl
from jax.experimental.pallas import tpu as pltpu

D_MODEL = 1024
DIFF_HEADS = 4
DIFF_HEAD_DIM = 64
DIFF_V_DIM = 128
DIFF_QK_COLS = DIFF_HEADS * 2 * DIFF_HEAD_DIM
DIFF_V_COLS = DIFF_HEADS * DIFF_V_DIM
MLA_HEADS = 4
MLA_Q_RANK = 384
MLA_KV_RANK = 256
MLA_NOPE_DIM = 128
MLA_ROPE_DIM = 64
MLA_V_DIM = 128
MLA_QK_PAD = 256
ROPE_BASE = 10000.0
D_FF = 4 * D_MODEL
REL_BUCKETS = 32
REL_MAX_DIST = 128
NORM_EPS = 1e-6
NEG_INF = -1e30
LOG2E = math.log2(math.e)
LAMBDA_INIT = 0.8 - 0.6 * math.exp(-0.3 * 0)

LANES = 128
VMEM_LIMIT = 56 * 1024 * 1024

NT_DIMS = (((1,), (1,)), ((), ()))


def _rms(x, gain):
    return x * lax.rsqrt(jnp.mean(x * x, axis=-1, keepdims=True) + NORM_EPS) * gain


def _rope_table_kernel(pos_ref, tab_ref):
    half = MLA_ROPE_DIM // 2
    lane = lax.broadcasted_iota(jnp.int32, (1, LANES), 1)
    fidx = (lane % half).astype(jnp.float32) * 2.0
    inv_freq = jnp.exp(-(fidx / MLA_ROPE_DIM) * math.log(ROPE_BASE))
    ang = pos_ref[...].astype(jnp.float32) * inv_freq
    c, s = jnp.cos(ang), jnp.sin(ang)
    tab_ref[...] = jnp.where(lane < 2 * half, c, jnp.where(lane < 3 * half, -s, s))


def _rope_table(positions, tile):
    seq = positions.shape[0]
    return pl.pallas_call(
        _rope_table_kernel,
        grid=(seq // tile,),
        in_specs=[pl.BlockSpec((tile, 1), lambda i: (i, 0))],
        out_specs=pl.BlockSpec((tile, LANES), lambda i: (i, 0)),
        out_shape=jax.ShapeDtypeStruct((seq, LANES), jnp.float32),
        name="rope_table",
    )(positions.reshape(seq, 1))


def _rope_slab(slab, tab):
    prod = slab * tab
    return prod + pltpu.roll(prod, MLA_ROPE_DIM, axis=1)


def _proj_kernel(x_ref, g_ref, w_in_ref, qn_ref, w_uq_ref, kvn_ref, w_ukv_ref, tab_ref,
                 dq_ref, dk_ref, dv_ref, mq_ref, mk_ref, mv_ref):
    bf = jnp.bfloat16
    h = _rms(x_ref[0], g_ref[...]).astype(bf)
    p = jnp.dot(h, w_in_ref[...], preferred_element_type=jnp.float32)
    tab = tab_ref[...]
    lane = lax.broadcasted_iota(jnp.int32, tab.shape, 1)

    dscale = (DIFF_HEAD_DIM ** -0.5) * LOG2E
    o_k, o_v = DIFF_QK_COLS, 2 * DIFF_QK_COLS
    for hh in range(DIFF_HEADS):
        sl = slice(hh * LANES, (hh + 1) * LANES)
        dq_ref[0, hh] = (p[:, sl] * dscale).astype(bf)
        dk_ref[0, hh] = p[:, o_k + hh * LANES:o_k + (hh + 1) * LANES].astype(bf)
        dv_ref[0, hh] = p[:, o_v + hh * LANES:o_v + (hh + 1) * LANES].astype(bf)

    o_cq = 2 * DIFF_QK_COLS + DIFF_V_COLS
    o_ckv = o_cq + MLA_Q_RANK
    o_kpe = o_ckv + MLA_KV_RANK
    cq = _rms(p[:, o_cq:o_ckv], qn_ref[...]).astype(bf)
    q = jnp.dot(cq, w_uq_ref[...], preferred_element_type=jnp.float32)
    ckv = _rms(p[:, o_ckv:o_kpe], kvn_ref[...]).astype(bf)
    kv = jnp.dot(ckv, w_ukv_ref[...], preferred_element_type=jnp.float32)
    k_rope = _rope_slab(p[:, o_kpe:o_kpe + LANES], tab).astype(bf)

    mscale = ((MLA_NOPE_DIM + MLA_ROPE_DIM) ** -0.5) * LOG2E
    for hh in range(MLA_HEADS):
        base = hh * MLA_QK_PAD
        q_rope = _rope_slab(q[:, base + LANES:base + 2 * LANES], tab)
        q_rope = jnp.where(lane < MLA_ROPE_DIM, q_rope, 0.0)
        mq_ref[0, hh, :, 0:LANES] = (q[:, base:base + LANES] * mscale).astype(bf)
        mq_ref[0, hh, :, LANES:2 * LANES] = (q_rope * mscale).astype(bf)
        mk_ref[0, hh, :, 0:LANES] = kv[:, base:base + LANES].astype(bf)
        mk_ref[0, hh, :, LANES:2 * LANES] = k_rope
        mv_ref[0, hh] = kv[:, base + LANES:base + 2 * LANES].astype(bf)


def _proj(x, g, w_in, qn, w_uq, kvn, w_ukv, tab, tile):
    B, S, D = x.shape
    bf = jnp.bfloat16
    const = lambda b, i: (0, 0)
    head_out = lambda w: pl.BlockSpec((1, DIFF_HEADS, tile, w), lambda b, i: (b, 0, i, 0))
    shp = lambda w: jax.ShapeDtypeStruct((B, DIFF_HEADS, S, w), bf)
    return pl.pallas_call(
        _proj_kernel,
        grid=(B, S // tile),
        in_specs=[
            pl.BlockSpec((1, tile, D), lambda b, i: (b, i, 0)),
            pl.BlockSpec(g.shape, const),
            pl.BlockSpec(w_in.shape, const),
            pl.BlockSpec(qn.shape, const),
            pl.BlockSpec(w_uq.shape, const),
            pl.BlockSpec(kvn.shape, const),
            pl.BlockSpec(w_ukv.shape, const),
            pl.BlockSpec((tile, LANES), lambda b, i: (i, 0)),
        ],
        out_specs=[head_out(LANES), head_out(LANES), head_out(LANES),
                   head_out(MLA_QK_PAD), head_out(MLA_QK_PAD), head_out(LANES)],
        out_shape=[shp(LANES), shp(LANES), shp(LANES),
                   shp(MLA_QK_PAD), shp(MLA_QK_PAD), shp(LANES)],
        compiler_params=pltpu.CompilerParams(
            dimension_semantics=("arbitrary", "arbitrary"), vmem_limit_bytes=VMEM_LIMIT),
        name="proj",
    )(x, g, w_in, qn, w_uq, kvn, w_ukv, tab)


def _schedule(positions, tq, tk, far):
    seq = positions.shape[0]
    qp = positions.reshape(seq // tq, tq)
    kp = positions.reshape(seq // tk, tk)
    qmin, qmax = qp.min(axis=1), qp.max(axis=1)
    kmin, kmax = kp.min(axis=1), kp.max(axis=1)
    skip = kmin[None, :] > qmax[:, None]
    interior = (qmin[:, None] - kmax[None, :]) >= far
    cls = jnp.where(interior, 0, jnp.where(skip, 2, 1)).astype(jnp.int32)
    order = jnp.argsort(cls, axis=1, stable=True).astype(jnp.int32)
    n_int = jnp.sum(cls == 0, axis=1).astype(jnp.int32)
    n_vis = jnp.sum(cls < 2, axis=1).astype(jnp.int32)
    return order.reshape(-1), n_int, n_vis


def _t5_bias_row(relb_ref, head):
    n = lax.broadcasted_iota(jnp.int32, (1, LANES), 1)
    max_exact = REL_BUCKETS // 2
    nf = jnp.maximum(n, 1).astype(jnp.float32)
    large = max_exact + (jnp.log(nf / max_exact) / math.log(REL_MAX_DIST / max_exact)
                         * (REL_BUCKETS - max_exact)).astype(jnp.int32)
    large = jnp.minimum(large, REL_BUCKETS - 1)
    bucket = jnp.where(n < max_exact, n, large)
    row = jnp.zeros((1, LANES), jnp.float32)
    for b in range(REL_BUCKETS):
        row = jnp.where(bucket == b, relb_ref[b, head] * LOG2E, row)
    return row


def _attn_kernel(*refs, tq, tk, nk, hp, diff):
    if diff:
        (tbl_ref, nint_ref, nvis_ref, relb_ref, q_ref, k_ref, v_ref, qn_ref, kn_ref,
         qpos_ref, kpos_ref, lq1_ref, lk1_ref, lq2_ref, lk2_ref, subln_ref,
         o_ref, m_ref, acc_ref, s0_ref, s1_ref, add_ref) = refs
    else:
        (tbl_ref, nint_ref, nvis_ref, q_ref, k_ref, v_ref, qn_ref, kn_ref, qpos_ref, kpos_ref,
         o_ref, m_ref, acc_ref, s0_ref, s1_ref, add_ref) = refs
    s_refs = (s0_ref, s1_ref)
    grp = pl.program_id(0)
    i = pl.program_id(1)
    batch = pl.program_id(2)
    n_vis = nvis_ref[i]
    n_int = jnp.minimum(nint_ref[i], n_vis - 1)
    start = (batch * n_vis) % 2
    nc = tk // LANES
    bf = jnp.bfloat16

    @pl.when(batch == 0)
    def _build_add_tiles():
        qpos = qpos_ref[...]
        if diff:
            tabs = [jnp.broadcast_to(_t5_bias_row(relb_ref, grp * hp + hh), (tq, LANES))
                    for hh in range(hp)]

        def fill(slot, carry):
            kpos = kpos_ref[tbl_ref[i * nk + n_int + slot]]
            for c in range(nc):
                cs = slice(c * LANES, (c + 1) * LANES)
                d = qpos - kpos[:, cs]
                if diff:
                    idx = jnp.clip(d, 0, LANES - 1)
                    for hh in range(hp):
                        add_ref[slot, hh, :, cs] = jnp.where(
                            d >= 0, jnp.take_along_axis(tabs[hh], idx, axis=1), NEG_INF)
                else:
                    add_ref[slot, 0, :, cs] = jnp.where(d >= 0, 0.0, NEG_INF)
            return carry

        lax.fori_loop(0, n_vis - n_int, fill, 0)

    def q_rows(ref, hh):
        q = ref[0, hh]
        if not diff:
            return q
        lane = lax.broadcasted_iota(jnp.int32, (tq, LANES), 1)
        zero = jnp.zeros_like(q)
        return jnp.concatenate([jnp.where(lane < DIFF_HEAD_DIM, q, zero),
                                jnp.where(lane >= DIFF_HEAD_DIM, q, zero)], axis=0)

    if diff:
        far_bias = [relb_ref[REL_BUCKETS - 1, grp * hp + hh] * LOG2E for hh in range(hp)]

    m_ref[...] = jnp.full(m_ref.shape, NEG_INF, jnp.float32)
    acc_ref[...] = jnp.zeros(acc_ref.shape, jnp.float32)

    def k_off(t):
        return pl.multiple_of(tbl_ref[i * nk + t] * tk, tk)

    mxu_n = 2 * LANES
    n_sub = tk // mxu_n

    def softmax(t, s, hh, interior):
        m_old = m_ref[hh]
        if interior:
            m_cur = jnp.max(s, axis=1, keepdims=True)
            if diff:
                m_new = jnp.maximum(m_old, m_cur + far_bias[hh])
                shift = m_new - far_bias[hh]
            else:
                m_new = jnp.maximum(m_old, m_cur)
                shift = m_new
        else:
            add = add_ref[t - n_int, hh if diff else 0]
            s = s + (jnp.concatenate([add, add], axis=0) if diff else add)
            m_new = jnp.maximum(m_old, jnp.max(s, axis=1, keepdims=True))
            shift = m_new
        m_ref[hh] = m_new
        p = jnp.concatenate([jnp.exp2(s[:, c * LANES:(c + 1) * LANES] - shift)
                             for c in range(nc)], axis=1).astype(bf)
        return p, jnp.exp2(m_old - m_new)

    def visit(t_sm, s_cur, interior, t_sc, s_nxt):
        nxt = isinstance(t_sc, str)
        off_sm = None if t_sm is None else k_off(t_sm)
        off_sc = None if (t_sc is None or nxt) else k_off(t_sc)
        for hh in range(hp):
            if t_sm is not None:
                p, alpha = softmax(t_sm, s_cur[hh], hh, interior)
                pv = None
            if t_sc is not None:
                q = q_rows(qn_ref if nxt else q_ref, hh)
            for c in range(n_sub):
                cs = slice(c * mxu_n, (c + 1) * mxu_n)
                if t_sc is not None:
                    if nxt:
                        k = kn_ref[0, hh, cs, :]
                    else:
                        k = k_ref[0, hh, pl.ds(off_sc + c * mxu_n, mxu_n), :]
                    s_nxt[hh, :, cs] = lax.dot_general(q, k, NT_DIMS,
                                                       preferred_element_type=jnp.float32)
                if t_sm is not None:
                    v = v_ref[0, hh, pl.ds(off_sm + c * mxu_n, mxu_n), :]
                    d = jnp.dot(p[:, cs], jnp.concatenate([v, jnp.ones_like(v)], axis=1),
                                preferred_element_type=jnp.float32)
                    pv = d if pv is None else pv + d
            if t_sm is not None:
                acc_ref[hh] = jnp.concatenate([alpha, alpha], axis=1) * acc_ref[hh] + pv

    def step(interior):
        def body(t, carry):
            for par in (0, 1):
                @pl.when((start + t) % 2 == par)
                def _():
                    visit(t, s_refs[par], interior, t + 1, s_refs[1 - par])
            return carry
        return body

    @pl.when(batch == 0)
    def _first_scores():
        visit(None, None, False, 0, s_refs[0])

    lax.fori_loop(0, n_int, step(True), 0)
    lax.fori_loop(n_int, n_vis - 1, step(False), 0)
    for par in (0, 1):
        @pl.when((start + n_vis - 1) % 2 == par)
        def _():
            visit(n_vis - 1, s_refs[par], False, "next", s_refs[1 - par])

    if diff:
        lam = (jnp.exp(jnp.sum(lq1_ref[...] * lk1_ref[...], axis=1, keepdims=True))
               - jnp.exp(jnp.sum(lq2_ref[...] * lk2_ref[...], axis=1, keepdims=True))
               + LAMBDA_INIT)
    for hh in range(hp):
        acc = acc_ref[hh]
        o = acc[:, :LANES] / acc[:, LANES:]
        if diff:
            oo = o[:tq] - lam * o[tq:]
            o = _rms(oo, subln_ref[...]) * (1.0 - LAMBDA_INIT)
        o_ref[0, :, hh * LANES:(hh + 1) * LANES] = o.astype(o_ref.dtype)


def _lane_rep(positions):
    return jnp.broadcast_to(positions[:, None], (positions.shape[0], LANES))


def _attention(q, k, v, positions, tq, tk, hp, diff_params=None):
    B, H, S, width = q.shape
    diff = diff_params is not None
    nq, nk = S // tq, S // tk
    rows = 2 * tq if diff else tq
    tbl, n_int, n_vis = _schedule(positions, tq, tk, LANES if diff else 0)
    imap = lambda f: (lambda g, i, b, *_: f(g, i, b))
    const = imap(lambda g, i, b: (0, 0))
    nb = lambda b: jnp.minimum(b + 1, B - 1)
    in_specs = [
        pl.BlockSpec((1, hp, tq, width), imap(lambda g, i, b: (b, g, i, 0))),
        pl.BlockSpec((1, hp, S, width), imap(lambda g, i, b: (b, g, 0, 0))),
        pl.BlockSpec((1, hp, S, LANES), imap(lambda g, i, b: (b, g, 0, 0))),
        pl.BlockSpec((1, hp, tq, width), imap(lambda g, i, b: (nb(b), g, i, 0))),
        pl.BlockSpec((1, hp, tk, width), lambda g, i, b, tbl, *_: (nb(b), g, tbl[i * nk], 0)),
        pl.BlockSpec((tq, LANES), imap(lambda g, i, b: (i, 0))),
        pl.BlockSpec((nk, 1, tk), imap(lambda g, i, b: (0, 0, 0))),
    ]
    prefetch = [tbl, n_int, n_vis]
    operands = [q, k, v, q, k, _lane_rep(positions), positions.reshape(nk, 1, tk)]
    if diff:
        rel_bias, lq1, lk1, lq2, lk2, subln = diff_params
        row = lambda a: a.reshape(1, -1).astype(jnp.float32)
        prefetch.append(rel_bias.astype(jnp.float32))
        operands += [row(lq1), row(lk1), row(lq2), row(lk2), row(subln)]
        in_specs += [pl.BlockSpec((1, DIFF_HEAD_DIM), const)] * 4 + [pl.BlockSpec((1, DIFF_V_DIM), const)]
    grid_spec = pltpu.PrefetchScalarGridSpec(
        num_scalar_prefetch=len(prefetch),
        grid=(H // hp, nq, B),
        in_specs=in_specs,
        out_specs=pl.BlockSpec((1, tq, hp * LANES), imap(lambda g, i, b: (b, i, g))),
        scratch_shapes=[pltpu.VMEM((hp, rows, LANES), jnp.float32),
                        pltpu.VMEM((hp, rows, 2 * LANES), jnp.float32),
                        pltpu.VMEM((hp, rows, tk), jnp.float32),
                        pltpu.VMEM((hp, rows, tk), jnp.float32),
                        pltpu.VMEM((nk, hp if diff else 1, tq, tk), jnp.float32)],
    )
    return pl.pallas_call(
        functools.partial(_attn_kernel, tq=tq, tk=tk, nk=nk, hp=hp, diff=diff),
        grid_spec=grid_spec,
        out_shape=jax.ShapeDtypeStruct((B, S, H * LANES), jnp.bfloat16),
        compiler_params=pltpu.CompilerParams(
            dimension_semantics=("arbitrary",) * 3, vmem_limit_bytes=VMEM_LIMIT),
        name="diff_attn" if diff else "mla_attn",
    )(*prefetch, *operands)


def _mlp_kernel(x_ref, ma_ref, mb_ref, wo_ref, g1_ref, w1_ref, w2_ref, g2_ref, o_ref, *, ff_chunk):
    bf = jnp.bfloat16
    n_a = ma_ref.shape[1]
    x1 = (x_ref[...]
          + jnp.dot(ma_ref[...], wo_ref[0:n_a, :], preferred_element_type=jnp.float32)
          + jnp.dot(mb_ref[...], wo_ref[n_a:, :], preferred_element_type=jnp.float32))
    h = _rms(x1, g1_ref[...]).astype(bf)
    y = jnp.zeros_like(x1)
    for c in range(D_FF // ff_chunk):
        sl = slice(c * ff_chunk, (c + 1) * ff_chunk)
        a = jnp.dot(h, w1_ref[:, sl], preferred_element_type=jnp.float32)
        a = jnp.square(jnp.maximum(a, 0.0)).astype(bf)
        y = y + jnp.dot(a, w2_ref[sl, :], preferred_element_type=jnp.float32)
    o_ref[...] = _rms(x1 + y, g2_ref[...])


def _mlp(x2d, mix_a, mix_b, w_out, g1, w1, w2, g2, tile, ff_chunk):
    N, D = x2d.shape
    const = lambda i: (0, 0)
    single = pl.Buffered(1)
    return pl.pallas_call(
        functools.partial(_mlp_kernel, ff_chunk=ff_chunk),
        grid=(N // tile,),
        in_specs=[
            pl.BlockSpec((tile, D), lambda i: (i, 0)),
            pl.BlockSpec((tile, mix_a.shape[1]), lambda i: (i, 0)),
            pl.BlockSpec((tile, mix_b.shape[1]), lambda i: (i, 0)),
            pl.BlockSpec(w_out.shape, const, pipeline_mode=single),
            pl.BlockSpec(g1.shape, const),
            pl.BlockSpec(w1.shape, const, pipeline_mode=single),
            pl.BlockSpec(w2.shape, const, pipeline_mode=single),
            pl.BlockSpec(g2.shape, const),
        ],
        out_specs=pl.BlockSpec((tile, D), lambda i: (i, 0)),
        out_shape=jax.ShapeDtypeStruct((N, D), jnp.float32),
        compiler_params=pltpu.CompilerParams(
            dimension_semantics=("arbitrary",), vmem_limit_bytes=VMEM_LIMIT),
        name="mlp",
    )(x2d, mix_a, mix_b, w_out, g1, w1, w2, g2)


def _swap_halves(w):
    half = w.shape[-1] // 2
    return jnp.concatenate([w[..., half:], w[..., :half]], axis=-1)


def kernel(x, positions, rel_bias, norm_attn, w_in, diff_lq1, diff_lk1, diff_lq2, diff_lk2,
           diff_subln, mla_q_norm, mla_w_uq, mla_kv_norm, mla_w_ukv, w_out, norm_mlp,
           w_mlp_in, w_mlp_out, norm_final):
    B, S, D = x.shape
    bf = jnp.bfloat16
    depth = w_in.shape[0]
    assert depth == 1
    l = 0
    row = lambda a: a.reshape(1, -1).astype(jnp.float32)

    w_in_l = w_in[l]
    k_pe_cols = w_in_l[:, -MLA_ROPE_DIM:]
    w_in_x = jnp.concatenate([w_in_l, _swap_halves(k_pe_cols)], axis=1).astype(bf)
    w_uq = mla_w_uq[l].reshape(MLA_Q_RANK, MLA_HEADS, MLA_NOPE_DIM + MLA_ROPE_DIM)
    q_pe_cols = w_uq[..., MLA_NOPE_DIM:]
    w_uq_x = jnp.concatenate([w_uq, _swap_halves(q_pe_cols)], axis=-1)
    w_uq_x = w_uq_x.reshape(MLA_Q_RANK, MLA_HEADS * MLA_QK_PAD).astype(bf)
    w_ukv = mla_w_ukv[l].astype(bf)

    tab = _rope_table(positions, 512)
    dq, dk, dv, mq, mk, mv = _proj(x, row(norm_attn[l]), w_in_x, row(mla_q_norm[l]), w_uq_x,
                                   row(mla_kv_norm[l]), w_ukv, tab, 512)
    mix_a = _attention(dq, dk, dv, positions, 256, 512, 4,
                       (rel_bias, diff_lq1[l], diff_lk1[l], diff_lq2[l], diff_lk2[l], diff_subln[l]))
    mix_b = _attention(mq, mk, mv, positions, 512, 512, 4)
    out = _mlp(x.reshape(B * S, D), mix_a.reshape(B * S, -1), mix_b.reshape(B * S, -1),
               w_out[l].astype(bf), row(norm_mlp[l]), w_mlp_in[l].astype(bf),
               w_mlp_out[l].astype(bf), row(norm_final), 512, 512)
    return out.reshape(B, S, D)
```

```python
import functools
import math

import jax
import jax.numpy as jnp
from jax import lax
from jax.experimental import pallas as pl
from jax.experimental.pallas import tpu as pltpu

D_MODEL = 1024
DIFF_HEADS = 4
DIFF_HEAD_DIM = 64
DIFF_V_DIM = 128
DIFF_QK_COLS = DIFF_HEADS * 2 * DIFF_HEAD_DIM
DIFF_V_COLS = DIFF_HEADS * DIFF_V_DIM
MLA_HEADS = 4
MLA_Q_RANK = 384
MLA_KV_RANK = 256
MLA_NOPE_DIM = 128
MLA_ROPE_DIM = 64
MLA_V_DIM = 128
MLA_QK_PAD = 256
ROPE_BASE = 10000.0
D_FF = 4 * D_MODEL
REL_BUCKETS = 32
REL_MAX_DIST = 128
NORM_EPS = 1e-6
NEG_INF = -1e30
LOG2E = math.log2(math.e)
LAMBDA_INIT = 0.8 - 0.6 * math.exp(-0.3 * 0)

LANES = 128
VMEM_LIMIT = 56 * 1024 * 1024

NT_DIMS = (((1,), (1,)), ((), ()))


def _rms(x, gain):
    return x * lax.rsqrt(jnp.mean(x * x, axis=-1, keepdims=True) + NORM_EPS) * gain


def _rope_table_kernel(pos_ref, tab_ref):
    half = MLA_ROPE_DIM // 2
    lane = lax.broadcasted_iota(jnp.int32, (1, LANES), 1)
    fidx = (lane % half).astype(jnp.float32) * 2.0
    inv_freq = jnp.exp(-(fidx / MLA_ROPE_DIM) * math.log(ROPE_BASE))
    ang = pos_ref[...].astype(jnp.float32) * inv_freq
    c, s = jnp.cos(ang), jnp.sin(ang)
    tab_ref[...] = jnp.where(lane < 2 * half, c, jnp.where(lane < 3 * half, -s, s))


def _rope_table(positions, tile):
    seq = positions.shape[0]
    return pl.pallas_call(
        _rope_table_kernel,
        grid=(seq // tile,),
        in_specs=[pl.BlockSpec((tile, 1), lambda i: (i, 0))],
        out_specs=pl.BlockSpec((tile, LANES), lambda i: (i, 0)),
        out_shape=jax.ShapeDtypeStruct((seq, LANES), jnp.float32),
        name="rope_table",
    )(positions.reshape(seq, 1))


def _rope_slab(slab, tab):
    prod = slab * tab
    return prod + pltpu.roll(prod, MLA_ROPE_DIM, axis=1)


def _proj_kernel(x_ref, g_ref, w_in_ref, qn_ref, w_uq_ref, kvn_ref, w_ukv_ref, tab_ref,
                 dq_ref, dk_ref, dv_ref, mq_ref, mk_ref, mv_ref):
    bf = jnp.bfloat16
    h = _rms(x_ref[0], g_ref[...]).astype(bf)
    o_ckv = MLA_Q_RANK
    o_kpe = o_ckv + MLA_KV_RANK
    o_dq = o_kpe + LANES
    o_dk = o_dq + DIFF_QK_COLS
    o_dv = o_dk + DIFF_QK_COLS
    p = jnp.dot(h, w_in_ref[:, :o_dq], preferred_element_type=jnp.float32)
    tab = tab_ref[...]
    lane = lax.broadcasted_iota(jnp.int32, tab.shape, 1)

    dscale = (DIFF_HEAD_DIM ** -0.5) * LOG2E
    pq = jnp.dot(h, w_in_ref[:, o_dq:o_dk], preferred_element_type=jnp.float32)
    for hh in range(DIFF_HEADS):
        dq_ref[0, hh] = (pq[:, hh * LANES:(hh + 1) * LANES] * dscale).astype(bf)

    cq = _rms(p[:, :o_ckv], qn_ref[...]).astype(bf)
    q = jnp.dot(cq, w_uq_ref[...], preferred_element_type=jnp.float32)
    ckv = _rms(p[:, o_ckv:o_kpe], kvn_ref[...]).astype(bf)
    kv = jnp.dot(ckv, w_ukv_ref[...], preferred_element_type=jnp.float32)
    k_rope = _rope_slab(p[:, o_kpe:o_dq], tab).astype(bf)

    pkv = jnp.dot(h, w_in_ref[:, o_dk:], preferred_element_type=jnp.float32)
    for hh in range(DIFF_HEADS):
        dk_ref[0, hh] = pkv[:, hh * LANES:(hh + 1) * LANES].astype(bf)
        dv_ref[0, hh] = pkv[:, DIFF_QK_COLS + hh * LANES:DIFF_QK_COLS + (hh + 1) * LANES].astype(bf)

    mscale = ((MLA_NOPE_DIM + MLA_ROPE_DIM) ** -0.5) * LOG2E
    for hh in range(MLA_HEADS):
        base = hh * MLA_QK_PAD
        q_rope = _rope_slab(q[:, base + LANES:base + 2 * LANES], tab)
        q_rope = jnp.where(lane < MLA_ROPE_DIM, q_rope, 0.0)
        mq_ref[0, hh, :, 0:LANES] = (q[:, base:base + LANES] * mscale).astype(bf)
        mq_ref[0, hh, :, LANES:2 * LANES] = (q_rope * mscale).astype(bf)
        mk_ref[0, hh, :, 0:LANES] = kv[:, base:base + LANES].astype(bf)
        mk_ref[0, hh, :, LANES:2 * LANES] = k_rope
        mv_ref[0, hh] = kv[:, base + LANES:base + 2 * LANES].astype(bf)


def _proj(x, g, w_in, qn, w_uq, kvn, w_ukv, tab, tile):
    B, S, D = x.shape
    bf = jnp.bfloat16
    const = lambda b, i: (0, 0)
    head_out = lambda w: pl.BlockSpec((1, DIFF_HEADS, tile, w), lambda b, i: (b, 0, i, 0))
    shp = lambda w: jax.ShapeDtypeStruct((B, DIFF_HEADS, S, w), bf)
    return pl.pallas_call(
        _proj_kernel,
        grid=(B, S // tile),
        in_specs=[
            pl.BlockSpec((1, tile, D), lambda b, i: (b, i, 0)),
            pl.BlockSpec(g.shape, const),
            pl.BlockSpec(w_in.shape, const),
            pl.BlockSpec(qn.shape, const),
            pl.BlockSpec(w_uq.shape, const),
            pl.BlockSpec(kvn.shape, const),
            pl.BlockSpec(w_ukv.shape, const),
            pl.BlockSpec((tile, LANES), lambda b, i: (i, 0)),
        ],
        out_specs=[head_out(LANES), head_out(LANES), head_out(LANES),
                   head_out(MLA_QK_PAD), head_out(MLA_QK_PAD), head_out(LANES)],
        out_shape=[shp(LANES), shp(LANES), shp(LANES),
                   shp(MLA_QK_PAD), shp(MLA_QK_PAD), shp(LANES)],
        compiler_params=pltpu.CompilerParams(
            dimension_semantics=("arbitrary", "arbitrary"), vmem_limit_bytes=VMEM_LIMIT),
        name="proj",
    )(x, g, w_in, qn, w_uq, kvn, w_ukv, tab)


def _schedule(positions, tq, tk, far):
    seq = positions.shape[0]
    qp = positions.reshape(seq // tq, tq)
    kp = positions.reshape(seq // tk, tk)
    qmin, qmax = qp.min(axis=1), qp.max(axis=1)
    kmin, kmax = kp.min(axis=1), kp.max(axis=1)
    skip = kmin[None, :] > qmax[:, None]
    interior = (qmin[:, None] - kmax[None, :]) >= far
    cls = jnp.where(interior, 0, jnp.where(skip, 2, 1)).astype(jnp.int32)
    order = jnp.argsort(cls, axis=1, stable=True).astype(jnp.int32)
    n_int = jnp.sum(cls == 0, axis=1).astype(jnp.int32)
    n_vis = jnp.sum(cls < 2, axis=1).astype(jnp.int32)
    return order.reshape(-1), n_int, n_vis


def _t5_bias_row(relb_ref, head):
    n = lax.broadcasted_iota(jnp.int32, (1, LANES), 1)
    max_exact = REL_BUCKETS // 2
    nf = jnp.maximum(n, 1).astype(jnp.float32)
    large = max_exact + (jnp.log(nf / max_exact) / math.log(REL_MAX_DIST / max_exact)
                         * (REL_BUCKETS - max_exact)).astype(jnp.int32)
    large = jnp.minimum(large, REL_BUCKETS - 1)
    bucket = jnp.where(n < max_exact, n, large)
    row = jnp.zeros((1, LANES), jnp.float32)
    for b in range(REL_BUCKETS):
        row = jnp.where(bucket == b, relb_ref[b, head] * LOG2E, row)
    return row


def _attn_kernel(*refs, tq, tk, nk, hp, diff):
    if diff:
        (tbl_ref, nint_ref, nvis_ref, relb_ref, q_ref, k_ref, v_ref, qn_ref, kn_ref,
         qpos_ref, kpos_ref, lq1_ref, lk1_ref, lq2_ref, lk2_ref, subln_ref,
         o_ref, m_ref, acc_ref, s0_ref, s1_ref, add_ref) = refs
    else:
        (tbl_ref, nint_ref, nvis_ref, q_ref, k_ref, v_ref, qn_ref, kn_ref, qpos_ref, kpos_ref,
         o_ref, m_ref, acc_ref, s0_ref, s1_ref, add_ref) = refs
    s_refs = (s0_ref, s1_ref)
    grp = pl.program_id(0)
    i = pl.program_id(1)
    batch = pl.program_id(2)
    n_vis = nvis_ref[i]
    n_int = jnp.minimum(nint_ref[i], n_vis - 1)
    start = (batch * n_vis) % 2
    nc = tk // LANES
    bf = jnp.bfloat16

    @pl.when(batch == 0)
    def _build_add_tiles():
        qpos = qpos_ref[...]
        if diff:
            tabs = [jnp.broadcast_to(_t5_bias_row(relb_ref, grp * hp + hh), (tq, LANES))
                    for hh in range(hp)]

        def fill(slot, carry):
            kpos = kpos_ref[tbl_ref[i * nk + n_int + slot]]
            for c in range(nc):
                cs = slice(c * LANES, (c + 1) * LANES)
                d = qpos - kpos[:, cs]
                if diff:
                    idx = jnp.clip(d, 0, LANES - 1)
                    for hh in range(hp):
                        add_ref[slot, hh, :, cs] = jnp.where(
                            d >= 0, jnp.take_along_axis(tabs[hh], idx, axis=1), NEG_INF)
                else:
                    add_ref[slot, 0, :, cs] = jnp.where(d >= 0, 0.0, NEG_INF)
            return carry

        lax.fori_loop(0, n_vis - n_int, fill, 0)

    def q_rows(ref, hh):
        q = ref[0, hh]
        if not diff:
            return q
        lane = lax.broadcasted_iota(jnp.int32, (tq, LANES), 1)
        zero = jnp.zeros_like(q)
        return jnp.concatenate([jnp.where(lane < DIFF_HEAD_DIM, q, zero),
                                jnp.where(lane >= DIFF_HEAD_DIM, q, zero)], axis=0)

    if diff:
        far_bias = [relb_ref[REL_BUCKETS - 1, grp * hp + hh] * LOG2E for hh in range(hp)]

    m_ref[...] = jnp.full(m_ref.shape, NEG_INF, jnp.float32)
    acc_ref[...] = jnp.zeros(acc_ref.shape, jnp.float32)

    def k_off(t):
        return pl.multiple_of(tbl_ref[i * nk + t] * tk, tk)

    mxu_n = 2 * LANES
    n_sub = tk // mxu_n

    def softmax(t, s, hh, interior):
        m_old = m_ref[hh]
        if interior:
            m_cur = jnp.max(s, axis=1, keepdims=True)
            if diff:
                m_new = jnp.maximum(m_old, m_cur + far_bias[hh])
                shift = m_new - far_bias[hh]
            else:
                m_new = jnp.maximum(m_old, m_cur)
                shift = m_new
        else:
            add = add_ref[t - n_int, hh if diff else 0]
            s = s + (jnp.concatenate([add, add], axis=0) if diff else add)
            m_new = jnp.maximum(m_old, jnp.max(s, axis=1, keepdims=True))
            shift = m_new
        m_ref[hh] = m_new
        p = jnp.concatenate([jnp.exp2(s[:, c * LANES:(c + 1) * LANES] - shift)
                             for c in range(nc)], axis=1).astype(bf)
        return p, jnp.exp2(m_old - m_new)

    def visit(t_sm, s_cur, interior, t_sc, s_nxt):
        nxt = isinstance(t_sc, str)
        off_sm = None if t_sm is None else k_off(t_sm)
        off_sc = None if (t_sc is None or nxt) else k_off(t_sc)
        sm, pvs, qrows = {}, {}, {}

        def score_dot(hh, c):
            cs = slice(c * mxu_n, (c + 1) * mxu_n)
            if hh not in qrows:
                qrows[hh] = q_rows(qn_ref if nxt else q_ref, hh)
            q = qrows[hh]
            if nxt:
                k = kn_ref[0, hh, cs, :]
            else:
                k = k_ref[0, hh, pl.ds(off_sc + c * mxu_n, mxu_n), :]
            s_nxt[hh, :, cs] = lax.dot_general(q, k, NT_DIMS, preferred_element_type=jnp.float32)

        def pv_dot(hh, c):
            cs = slice(c * mxu_n, (c + 1) * mxu_n)
            v = v_ref[0, hh, pl.ds(off_sm + c * mxu_n, mxu_n), :]
            d = jnp.dot(sm[hh][0][:, cs], jnp.concatenate([v, jnp.ones_like(v)], axis=1),
                        preferred_element_type=jnp.float32)
            pvs[hh] = d if hh not in pvs else pvs[hh] + d

        tiles = [(hh, c) for hh in range(hp) for c in range(n_sub)]
        for hh, c in tiles:
            if t_sm is not None and hh not in sm:
                sm[hh] = softmax(t_sm, s_cur[hh], hh, interior)
            if t_sc is not None:
                score_dot(hh, c)
            if t_sm is not None:
                pv_dot(hh, c)
                if c == n_sub - 1:
                    alpha = sm[hh][1]
                    acc_ref[hh] = jnp.concatenate([alpha, alpha], axis=1) * acc_ref[hh] + pvs[hh]

    def step(interior):
        def body(t, carry):
            for par in (0, 1):
                @pl.when((start + t) % 2 == par)
                def _():
                    visit(t, s_refs[par], interior, t + 1, s_refs[1 - par])
            return carry
        return body

    @pl.when(batch == 0)
    def _first_scores():
        visit(None, None, False, 0, s_refs[0])

    lax.fori_loop(0, n_int, step(True), 0)
    lax.fori_loop(n_int, n_vis - 1, step(False), 0)
    for par in (0, 1):
        @pl.when((start + n_vis - 1) % 2 == par)
        def _():
            visit(n_vis - 1, s_refs[par], False, "next", s_refs[1 - par])

    if diff:
        lam = (jnp.exp(jnp.sum(lq1_ref[...] * lk1_ref[...], axis=1, keepdims=True))
               - jnp.exp(jnp.sum(lq2_ref[...] * lk2_ref[...], axis=1, keepdims=True))
               + LAMBDA_INIT)
    for hh in range(hp):
        acc = acc_ref[hh]
        o = acc[:, :LANES] / acc[:, LANES:]
        if diff:
            oo = o[:tq] - lam * o[tq:]
            o = _rms(oo, subln_ref[...]) * (1.0 - LAMBDA_INIT)
        o_ref[0, :, hh * LANES:(hh + 1) * LANES] = o.astype(o_ref.dtype)


def _lane_rep(positions):
    return jnp.broadcast_to(positions[:, None], (positions.shape[0], LANES))


def _attention(q, k, v, positions, tq, tk, hp, diff_params=None):
    B, H, S, width = q.shape
    diff = diff_params is not None
    nq, nk = S // tq, S // tk
    rows = 2 * tq if diff else tq
    tbl, n_int, n_vis = _schedule(positions, tq, tk, LANES if diff else 0)
    imap = lambda f: (lambda g, i, b, *_: f(g, i, b))
    const = imap(lambda g, i, b: (0, 0))
    nb = lambda b: jnp.minimum(b + 1, B - 1)
    in_specs = [
        pl.BlockSpec((1, hp, tq, width), imap(lambda g, i, b: (b, g, i, 0))),
        pl.BlockSpec((1, hp, S, width), imap(lambda g, i, b: (b, g, 0, 0))),
        pl.BlockSpec((1, hp, S, LANES), imap(lambda g, i, b: (b, g, 0, 0))),
        pl.BlockSpec((1, hp, tq, width), imap(lambda g, i, b: (nb(b), g, i, 0))),
        pl.BlockSpec((1, hp, tk, width), lambda g, i, b, tbl, *_: (nb(b), g, tbl[i * nk], 0)),
        pl.BlockSpec((tq, LANES), imap(lambda g, i, b: (i, 0))),
        pl.BlockSpec((nk, 1, tk), imap(lambda g, i, b: (0, 0, 0))),
    ]
    prefetch = [tbl, n_int, n_vis]
    operands = [q, k, v, q, k, _lane_rep(positions), positions.reshape(nk, 1, tk)]
    if diff:
        rel_bias, lq1, lk1, lq2, lk2, subln = diff_params
        row = lambda a: a.reshape(1, -1).astype(jnp.float32)
        prefetch.append(rel_bias.astype(jnp.float32))
        operands += [row(lq1), row(lk1), row(lq2), row(lk2), row(subln)]
        in_specs += [pl.BlockSpec((1, DIFF_HEAD_DIM), const)] * 4 + [pl.BlockSpec((1, DIFF_V_DIM), const)]
    grid_spec = pltpu.PrefetchScalarGridSpec(
        num_scalar_prefetch=len(prefetch),
        grid=(H // hp, nq, B),
        in_specs=in_specs,
        out_specs=pl.BlockSpec((1, tq, hp * LANES), imap(lambda g, i, b: (b, i, g))),
        scratch_shapes=[pltpu.VMEM((hp, rows, LANES), jnp.float32),
                        pltpu.VMEM((hp, rows, 2 * LANES), jnp.float32),
                        pltpu.VMEM((hp, rows, tk), jnp.float32),
                        pltpu.VMEM((hp, rows, tk), jnp.float32),
                        pltpu.VMEM((nk, hp if diff else 1, tq, tk), jnp.float32)],
    )
    return pl.pallas_call(
        functools.partial(_attn_kernel, tq=tq, tk=tk, nk=nk, hp=hp, diff=diff),
        grid_spec=grid_spec,
        out_shape=jax.ShapeDtypeStruct((B, S, H * LANES), jnp.bfloat16),
        compiler_params=pltpu.CompilerParams(
            dimension_semantics=("arbitrary",) * 3, vmem_limit_bytes=VMEM_LIMIT),
        name="diff_attn" if diff else "mla_attn",
    )(*prefetch, *operands)


def _mlp_kernel(x_ref, ma_ref, mb_ref, wo_ref, g1_ref, w1_ref, w2_ref, g2_ref, o_ref, *,
                ff_chunk, row_chunk):
    bf = jnp.bfloat16
    n_a = ma_ref.shape[1]
    n_ff = D_FF // ff_chunk

    def head(rs):
        x1 = (x_ref[rs, :]
              + jnp.dot(ma_ref[rs, :], wo_ref[0:n_a, :], preferred_element_type=jnp.float32)
              + jnp.dot(mb_ref[rs, :], wo_ref[n_a:, :], preferred_element_type=jnp.float32))
        return x1, _rms(x1, g1_ref[...]).astype(bf)

    def ff(h, y, c):
        sl = slice(c * ff_chunk, (c + 1) * ff_chunk)
        a = jnp.dot(h, w1_ref[:, sl], preferred_element_type=jnp.float32)
        a = jnp.square(jnp.maximum(a, 0.0)).astype(bf)
        d = jnp.dot(a, w2_ref[sl, :], preferred_element_type=jnp.float32)
        return d if y is None else y + d

    chunks = [slice(r * row_chunk, (r + 1) * row_chunk) for r in range(x_ref.shape[0] // row_chunk)]
    x1, h = head(chunks[0])
    for r, rs in enumerate(chunks):
        y = None if r == 0 else y_first
        for c in range(0 if r == 0 else 1, n_ff - 1):
            y = ff(h, y, c)
        if r + 1 < len(chunks):
            x1_n, h_n = head(chunks[r + 1])
        y = ff(h, y, n_ff - 1)
        if r + 1 < len(chunks):
            y_first = ff(h_n, None, 0)
        o_ref[rs, :] = _rms(x1 + y, g2_ref[...])
        if r + 1 < len(chunks):
            x1, h = x1_n, h_n


def _mlp(x2d, mix_a, mix_b, w_out, g1, w1, w2, g2, tile, ff_chunk, row_chunk):
    N, D = x2d.shape
    const = lambda i: (0, 0)
    single = pl.Buffered(1)
    return pl.pallas_call(
        functools.partial(_mlp_kernel, ff_chunk=ff_chunk, row_chunk=row_chunk),
        grid=(N // tile,),
        in_specs=[
            pl.BlockSpec((tile, D), lambda i: (i, 0)),
            pl.BlockSpec((tile, mix_a.shape[1]), lambda i: (i, 0)),
            pl.BlockSpec((tile, mix_b.shape[1]), lambda i: (i, 0)),
            pl.BlockSpec(w_out.shape, const, pipeline_mode=single),
            pl.BlockSpec(g1.shape, const),
            pl.BlockSpec(w1.shape, const, pipeline_mode=single),
            pl.BlockSpec(w2.shape, const, pipeline_mode=single),
            pl.BlockSpec(g2.shape, const),
        ],
        out_specs=pl.BlockSpec((tile, D), lambda i: (i, 0)),
        out_shape=jax.ShapeDtypeStruct((N, D), jnp.float32),
        compiler_params=pltpu.CompilerParams(
            dimension_semantics=("arbitrary",), vmem_limit_bytes=VMEM_LIMIT),
        name="mlp",
    )(x2d, mix_a, mix_b, w_out, g1, w1, w2, g2)


def _swap_halves(w):
    half = w.shape[-1] // 2
    return jnp.concatenate([w[..., half:], w[..., :half]], axis=-1)


def kernel(x, positions, rel_bias, norm_attn, w_in, diff_lq1, diff_lk1, diff_lq2, diff_lk2,
           diff_subln, mla_q_norm, mla_w_uq, mla_kv_norm, mla_w_ukv, w_out, norm_mlp,
           w_mlp_in, w_mlp_out, norm_final):
    B, S, D = x.shape
    bf = jnp.bfloat16
    depth = w_in.shape[0]
    assert depth == 1
    l = 0
    row = lambda a: a.reshape(1, -1).astype(jnp.float32)

    w_in_l = w_in[l]
    n_diff = 2 * DIFF_QK_COLS + DIFF_V_COLS
    k_pe_cols = w_in_l[:, -MLA_ROPE_DIM:]
    w_in_x = jnp.concatenate([w_in_l[:, n_diff:], _swap_halves(k_pe_cols), w_in_l[:, :n_diff]],
                             axis=1).astype(bf)
    w_uq = mla_w_uq[l].reshape(MLA_Q_RANK, MLA_HEADS, MLA_NOPE_DIM + MLA_ROPE_DIM)
    q_pe_cols = w_uq[..., MLA_NOPE_DIM:]
    w_uq_x = jnp.concatenate([w_uq, _swap_halves(q_pe_cols)], axis=-1)
    w_uq_x = w_uq_x.reshape(MLA_Q_RANK, MLA_HEADS * MLA_QK_PAD).astype(bf)
    w_ukv = mla_w_ukv[l].astype(bf)

    tab = _rope_table(positions, 512)
    dq, dk, dv, mq, mk, mv = _proj(x, row(norm_attn[l]), w_in_x, row(mla_q_norm[l]), w_uq_x,
                                   row(mla_kv_norm[l]), w_ukv, tab, 512)
    mix_a = _attention(dq, dk, dv, positions, 256, 512, 4,
                       (rel_bias, diff_lq1[l], diff_lk1[l], diff_lq2[l], diff_lk2[l], diff_subln[l]))
    mix_b = _attention(mq, mk, mv, positions, 512, 512, 4)
    out = _mlp(x.reshape(B * S, D), mix_a.reshape(B * S, -1), mix_b.reshape(B * S, -1),
               w_out[l].astype(bf), row(norm_mlp[l]), w_mlp_in[l].astype(bf),
               w_mlp_out[l].astype(bf), row(norm_final), 1024, 512, 512)
    return out.reshape(B, S, D)
```

```python
import functools
import math

import jax
import jax.numpy as jnp
from jax import lax
from jax.experimental import pallas as pl
from jax.experimental.pallas import tpu as pltpu

D_MODEL = 1024
DIFF_HEADS = 4
DIFF_HEAD_DIM = 64
DIFF_V_DIM = 128
DIFF_QK_COLS = DIFF_HEADS * 2 * DIFF_HEAD_DIM
DIFF_V_COLS = DIFF_HEADS * DIFF_V_DIM
MLA_HEADS = 4
MLA_Q_RANK = 384
MLA_KV_RANK = 256
MLA_NOPE_DIM = 128
MLA_ROPE_DIM = 64
MLA_V_DIM = 128
MLA_QK_PAD = 256
ROPE_BASE = 10000.0
D_FF = 4 * D_MODEL
REL_BUCKETS = 32
REL_MAX_DIST = 128
NORM_EPS = 1e-6
NEG_INF = -1e30
LOG2E = math.log2(math.e)
LAMBDA_INIT = 0.8 - 0.6 * math.exp(-0.3 * 0)

LANES = 128
VMEM_LIMIT = 56 * 1024 * 1024

NT_DIMS = (((1,), (1,)), ((), ()))


def _rms(x, gain):
    return x * lax.rsqrt(jnp.mean(x * x, axis=-1, keepdims=True) + NORM_EPS) * gain


def _rope_table_kernel(pos_ref, tab_ref):
    half = MLA_ROPE_DIM // 2
    lane = lax.broadcasted_iota(jnp.int32, (1, LANES), 1)
    fidx = (lane % half).astype(jnp.float32) * 2.0
    inv_freq = jnp.exp(-(fidx / MLA_ROPE_DIM) * math.log(ROPE_BASE))
    ang = pos_ref[...].astype(jnp.float32) * inv_freq
    c, s = jnp.cos(ang), jnp.sin(ang)
    tab_ref[...] = jnp.where(lane < 2 * half, c, jnp.where(lane < 3 * half, -s, s))


def _rope_table(positions, tile):
    seq = positions.shape[0]
    return pl.pallas_call(
        _rope_table_kernel,
        grid=(seq // tile,),
        in_specs=[pl.BlockSpec((tile, 1), lambda i: (i, 0))],
        out_specs=pl.BlockSpec((tile, LANES), lambda i: (i, 0)),
        out_shape=jax.ShapeDtypeStruct((seq, LANES), jnp.float32),
        name="rope_table",
    )(positions.reshape(seq, 1))


def _rope_slab(slab, tab):
    prod = slab * tab
    return prod + pltpu.roll(prod, MLA_ROPE_DIM, axis=1)


def _proj_kernel(x_ref, g_ref, w_in_ref, qn_ref, w_uq_ref, kvn_ref, w_ukv_ref, tab_ref,
                 dq_ref, dk_ref, dv_ref, mq_ref, mk_ref, mv_ref):
    bf = jnp.bfloat16
    x = x_ref[0]
    h = (x * g_ref[...]).astype(bf)
    r = lax.rsqrt(jnp.mean(x * x, axis=-1, keepdims=True) + NORM_EPS)
    o_ckv = MLA_Q_RANK
    o_kpe = o_ckv + MLA_KV_RANK
    o_dq = o_kpe + LANES
    o_dk = o_dq + DIFF_QK_COLS
    o_dv = o_dk + DIFF_QK_COLS
    p = jnp.dot(h, w_in_ref[:, :o_dq], preferred_element_type=jnp.float32) * r
    tab = tab_ref[...]
    lane = lax.broadcasted_iota(jnp.int32, tab.shape, 1)

    dscale = (DIFF_HEAD_DIM ** -0.5) * LOG2E
    pq = jnp.dot(h, w_in_ref[:, o_dq:o_dk], preferred_element_type=jnp.float32)
    rq = r * dscale
    for hh in range(DIFF_HEADS):
        dq_ref[0, hh] = (pq[:, hh * LANES:(hh + 1) * LANES] * rq).astype(bf)

    cq = _rms(p[:, :o_ckv], qn_ref[...]).astype(bf)
    q = jnp.dot(cq, w_uq_ref[...], preferred_element_type=jnp.float32)
    ckv = _rms(p[:, o_ckv:o_kpe], kvn_ref[...]).astype(bf)
    kv = jnp.dot(ckv, w_ukv_ref[...], preferred_element_type=jnp.float32)
    k_rope = _rope_slab(p[:, o_kpe:o_dq], tab).astype(bf)

    pkv = jnp.dot(h, w_in_ref[:, o_dk:], preferred_element_type=jnp.float32)
    for hh in range(DIFF_HEADS):
        dk_ref[0, hh] = (pkv[:, hh * LANES:(hh + 1) * LANES] * r).astype(bf)
        dv_ref[0, hh] = (pkv[:, DIFF_QK_COLS + hh * LANES:DIFF_QK_COLS + (hh + 1) * LANES]
                         * r).astype(bf)

    mscale = ((MLA_NOPE_DIM + MLA_ROPE_DIM) ** -0.5) * LOG2E
    for hh in range(MLA_HEADS):
        base = hh * MLA_QK_PAD
        q_rope = _rope_slab(q[:, base + LANES:base + 2 * LANES], tab)
        q_rope = jnp.where(lane < MLA_ROPE_DIM, q_rope, 0.0)
        mq_ref[0, hh, :, 0:LANES] = (q[:, base:base + LANES] * mscale).astype(bf)
        mq_ref[0, hh, :, LANES:2 * LANES] = (q_rope * mscale).astype(bf)
        mk_ref[0, hh, :, 0:LANES] = kv[:, base:base + LANES].astype(bf)
        mk_ref[0, hh, :, LANES:2 * LANES] = k_rope
        mv_ref[0, hh] = kv[:, base + LANES:base + 2 * LANES].astype(bf)


def _proj(x, g, w_in, qn, w_uq, kvn, w_ukv, tab, tile):
    B, S, D = x.shape
    bf = jnp.bfloat16
    const = lambda b, i: (0, 0)
    head_out = lambda w: pl.BlockSpec((1, DIFF_HEADS, tile, w), lambda b, i: (b, 0, i, 0))
    shp = lambda w: jax.ShapeDtypeStruct((B, DIFF_HEADS, S, w), bf)
    return pl.pallas_call(
        _proj_kernel,
        grid=(B, S // tile),
        in_specs=[
            pl.BlockSpec((1, tile, D), lambda b, i: (b, i, 0)),
            pl.BlockSpec(g.shape, const),
            pl.BlockSpec(w_in.shape, const),
            pl.BlockSpec(qn.shape, const),
            pl.BlockSpec(w_uq.shape, const),
            pl.BlockSpec(kvn.shape, const),
            pl.BlockSpec(w_ukv.shape, const),
            pl.BlockSpec((tile, LANES), lambda b, i: (i, 0)),
        ],
        out_specs=[head_out(LANES), head_out(LANES), head_out(LANES),
                   head_out(MLA_QK_PAD), head_out(MLA_QK_PAD), head_out(LANES)],
        out_shape=[shp(LANES), shp(LANES), shp(LANES),
                   shp(MLA_QK_PAD), shp(MLA_QK_PAD), shp(LANES)],
        compiler_params=pltpu.CompilerParams(
            dimension_semantics=("arbitrary", "arbitrary"), vmem_limit_bytes=VMEM_LIMIT),
        name="proj",
    )(x, g, w_in, qn, w_uq, kvn, w_ukv, tab)


def _schedule(positions, tq, tk, far):
    seq = positions.shape[0]
    qp = positions.reshape(seq // tq, tq)
    kp = positions.reshape(seq // tk, tk)
    qmin, qmax = qp.min(axis=1), qp.max(axis=1)
    kmin, kmax = kp.min(axis=1), kp.max(axis=1)
    skip = kmin[None, :] > qmax[:, None]
    interior = (qmin[:, None] - kmax[None, :]) >= far
    cls = jnp.where(interior, 0, jnp.where(skip, 2, 1)).astype(jnp.int32)
    order = jnp.argsort(cls, axis=1, stable=True).astype(jnp.int32)
    n_int = jnp.sum(cls == 0, axis=1).astype(jnp.int32)
    n_vis = jnp.sum(cls < 2, axis=1).astype(jnp.int32)
    return order.reshape(-1), n_int, n_vis


def _t5_bias_row(relb_ref, head):
    n = lax.broadcasted_iota(jnp.int32, (1, LANES), 1)
    max_exact = REL_BUCKETS // 2
    nf = jnp.maximum(n, 1).astype(jnp.float32)
    large = max_exact + (jnp.log(nf / max_exact) / math.log(REL_MAX_DIST / max_exact)
                         * (REL_BUCKETS - max_exact)).astype(jnp.int32)
    large = jnp.minimum(large, REL_BUCKETS - 1)
    bucket = jnp.where(n < max_exact, n, large)
    row = jnp.zeros((1, LANES), jnp.float32)
    for b in range(REL_BUCKETS):
        row = jnp.where(bucket == b, relb_ref[b, head] * LOG2E, row)
    return row


def _attn_kernel(*refs, tq, tk, nk, hp, diff):
    if diff:
        (tbl_ref, nint_ref, nvis_ref, relb_ref, q_ref, k_ref, v_ref, qn_ref, kn_ref,
         qpos_ref, kpos_ref, lq1_ref, lk1_ref, lq2_ref, lk2_ref, subln_ref,
         o_ref, m_ref, acc_ref, s0_ref, s1_ref, add_ref) = refs
    else:
        (tbl_ref, nint_ref, nvis_ref, q_ref, k_ref, v_ref, qn_ref, kn_ref, qpos_ref, kpos_ref,
         o_ref, m_ref, acc_ref, s0_ref, s1_ref, add_ref) = refs
    s_refs = (s0_ref, s1_ref)
    grp = pl.program_id(0)
    i = pl.program_id(1)
    batch = pl.program_id(2)
    n_vis = nvis_ref[i]
    n_int = jnp.minimum(nint_ref[i], n_vis - 1)
    start = (batch * n_vis) % 2
    nc = tk // LANES
    bf = jnp.bfloat16

    @pl.when(batch == 0)
    def _build_add_tiles():
        qpos = qpos_ref[...]
        if diff:
            tabs = [jnp.broadcast_to(_t5_bias_row(relb_ref, grp * hp + hh), (tq, LANES))
                    for hh in range(hp)]

        def fill(slot, carry):
            kpos = kpos_ref[tbl_ref[i * nk + n_int + slot]]
            for c in range(nc):
                cs = slice(c * LANES, (c + 1) * LANES)
                d = qpos - kpos[:, cs]
                if diff:
                    idx = jnp.clip(d, 0, LANES - 1)
                    for hh in range(hp):
                        add_ref[slot, hh, :, cs] = jnp.where(
                            d >= 0, jnp.take_along_axis(tabs[hh], idx, axis=1), NEG_INF)
                else:
                    add_ref[slot, 0, :, cs] = jnp.where(d >= 0, 0.0, NEG_INF)
            return carry

        lax.fori_loop(0, n_vis - n_int, fill, 0)

    def q_rows(ref, hh):
        q = ref[0, hh]
        if not diff:
            return q
        lane = lax.broadcasted_iota(jnp.int32, (tq, LANES), 1)
        zero = jnp.zeros_like(q)
        return jnp.concatenate([jnp.where(lane < DIFF_HEAD_DIM, q, zero),
                                jnp.where(lane >= DIFF_HEAD_DIM, q, zero)], axis=0)

    if diff:
        far_bias = [relb_ref[REL_BUCKETS - 1, grp * hp + hh] * LOG2E for hh in range(hp)]

    m_ref[...] = jnp.full(m_ref.shape, NEG_INF, jnp.float32)
    acc_ref[...] = jnp.zeros(acc_ref.shape, jnp.float32)

    def k_off(t):
        return pl.multiple_of(tbl_ref[i * nk + t] * tk, tk)

    mxu_n = 2 * LANES
    n_sub = tk // mxu_n

    def softmax(t, s, hh, interior):
        m_old = m_ref[hh]
        if interior:
            m_cur = jnp.max(s, axis=1, keepdims=True)
            if diff:
                m_new = jnp.maximum(m_old, m_cur + far_bias[hh])
                shift = m_new - far_bias[hh]
            else:
                m_new = jnp.maximum(m_old, m_cur)
                shift = m_new
        else:
            add = add_ref[t - n_int, hh if diff else 0]
            s = s + (jnp.concatenate([add, add], axis=0) if diff else add)
            m_new = jnp.maximum(m_old, jnp.max(s, axis=1, keepdims=True))
            shift = m_new
        m_ref[hh] = m_new
        p = jnp.concatenate([jnp.exp2(s[:, c * LANES:(c + 1) * LANES] - shift)
                             for c in range(nc)], axis=1).astype(bf)
        return p, jnp.exp2(m_old - m_new)

    def visit(t_sm, s_cur, interior, t_sc, s_nxt):
        nxt = isinstance(t_sc, str)
        off_sm = None if t_sm is None else k_off(t_sm)
        off_sc = None if (t_sc is None or nxt) else k_off(t_sc)
        sm, pvs, qrows = {}, {}, {}

        def score_dot(hh, c):
            cs = slice(c * mxu_n, (c + 1) * mxu_n)
            if hh not in qrows:
                qrows[hh] = q_rows(qn_ref if nxt else q_ref, hh)
            q = qrows[hh]
            if nxt:
                k = kn_ref[0, hh, cs, :]
            else:
                k = k_ref[0, hh, pl.ds(off_sc + c * mxu_n, mxu_n), :]
            s_nxt[hh, :, cs] = lax.dot_general(q, k, NT_DIMS, preferred_element_type=jnp.float32)

        def pv_dot(hh, c):
            cs = slice(c * mxu_n, (c + 1) * mxu_n)
            v = v_ref[0, hh, pl.ds(off_sm + c * mxu_n, mxu_n), :]
            d = jnp.dot(sm[hh][0][:, cs], jnp.concatenate([v, jnp.ones_like(v)], axis=1),
                        preferred_element_type=jnp.float32)
            pvs[hh] = d if hh not in pvs else pvs[hh] + d

        tiles = [(hh, c) for hh in range(hp) for c in range(n_sub)]
        for hh, c in tiles:
            if t_sm is not None and hh not in sm:
                sm[hh] = softmax(t_sm, s_cur[hh], hh, interior)
            if t_sc is not None:
                score_dot(hh, c)
            if t_sm is not None:
                pv_dot(hh, c)
                if c == n_sub - 1:
                    alpha = sm[hh][1]
                    acc_ref[hh] = jnp.concatenate([alpha, alpha], axis=1) * acc_ref[hh] + pvs[hh]

    def step(interior):
        def body(t, carry):
            for par in (0, 1):
                @pl.when((start + t) % 2 == par)
                def _():
                    visit(t, s_refs[par], interior, t + 1, s_refs[1 - par])
            return carry
        return body

    @pl.when(batch == 0)
    def _first_scores():
        visit(None, None, False, 0, s_refs[0])

    lax.fori_loop(0, n_int, step(True), 0)
    lax.fori_loop(n_int, n_vis - 1, step(False), 0)
    for par in (0, 1):
        @pl.when((start + n_vis - 1) % 2 == par)
        def _():
            visit(n_vis - 1, s_refs[par], False, "next", s_refs[1 - par])

    if diff:
        lam = (jnp.exp(jnp.sum(lq1_ref[...] * lk1_ref[...], axis=1, keepdims=True))
               - jnp.exp(jnp.sum(lq2_ref[...] * lk2_ref[...], axis=1, keepdims=True))
               + LAMBDA_INIT)
    for hh in range(hp):
        acc = acc_ref[hh]
        o = acc[:, :LANES] / acc[:, LANES:]
        if diff:
            oo = o[:tq] - lam * o[tq:]
            o = _rms(oo, subln_ref[...]) * (1.0 - LAMBDA_INIT)
        o_ref[0, :, hh * LANES:(hh + 1) * LANES] = o.astype(o_ref.dtype)


def _lane_rep(positions):
    return jnp.broadcast_to(positions[:, None], (positions.shape[0], LANES))


def _attention(q, k, v, positions, tq, tk, hp, diff_params=None):
    B, H, S, width = q.shape
    diff = diff_params is not None
    nq, nk = S // tq, S // tk
    rows = 2 * tq if diff else tq
    tbl, n_int, n_vis = _schedule(positions, tq, tk, LANES if diff else 0)
    imap = lambda f: (lambda g, i, b, *_: f(g, i, b))
    const = imap(lambda g, i, b: (0, 0))
    nb = lambda b: jnp.minimum(b + 1, B - 1)
    in_specs = [
        pl.BlockSpec((1, hp, tq, width), imap(lambda g, i, b: (b, g, i, 0))),
        pl.BlockSpec((1, hp, S, width), imap(lambda g, i, b: (b, g, 0, 0))),
        pl.BlockSpec((1, hp, S, LANES), imap(lambda g, i, b: (b, g, 0, 0))),
        pl.BlockSpec((1, hp, tq, width), imap(lambda g, i, b: (nb(b), g, i, 0))),
        pl.BlockSpec((1, hp, tk, width), lambda g, i, b, tbl, *_: (nb(b), g, tbl[i * nk], 0)),
        pl.BlockSpec((tq, LANES), imap(lambda g, i, b: (i, 0))),
        pl.BlockSpec((nk, 1, tk), imap(lambda g, i, b: (0, 0, 0))),
    ]
    prefetch = [tbl, n_int, n_vis]
    operands = [q, k, v, q, k, _lane_rep(positions), positions.reshape(nk, 1, tk)]
    if diff:
        rel_bias, lq1, lk1, lq2, lk2, subln = diff_params
        row = lambda a: a.reshape(1, -1).astype(jnp.float32)
        prefetch.append(rel_bias.astype(jnp.float32))
        operands += [row(lq1), row(lk1), row(lq2), row(lk2), row(subln)]
        in_specs += [pl.BlockSpec((1, DIFF_HEAD_DIM), const)] * 4 + [pl.BlockSpec((1, DIFF_V_DIM), const)]
    grid_spec = pltpu.PrefetchScalarGridSpec(
        num_scalar_prefetch=len(prefetch),
        grid=(H // hp, nq, B),
        in_specs=in_specs,
        out_specs=pl.BlockSpec((1, tq, hp * LANES), imap(lambda g, i, b: (b, i, g))),
        scratch_shapes=[pltpu.VMEM((hp, rows, LANES), jnp.float32),
                        pltpu.VMEM((hp, rows, 2 * LANES), jnp.float32),
                        pltpu.VMEM((hp, rows, tk), jnp.float32),
                        pltpu.VMEM((hp, rows, tk), jnp.float32),
                        pltpu.VMEM((nk, hp if diff else 1, tq, tk), jnp.float32)],
    )
    return pl.pallas_call(
        functools.partial(_attn_kernel, tq=tq, tk=tk, nk=nk, hp=hp, diff=diff),
        grid_spec=grid_spec,
        out_shape=jax.ShapeDtypeStruct((B, S, H * LANES), jnp.bfloat16),
        compiler_params=pltpu.CompilerParams(
            dimension_semantics=("arbitrary",) * 3, vmem_limit_bytes=VMEM_LIMIT),
        name="diff_attn" if diff else "mla_attn",
    )(*prefetch, *operands)


def _mlp_kernel(x_ref, ma_ref, mb_ref, wo_ref, g1_ref, w1_ref, w2_ref, g2_ref, o_ref, *,
                ff_chunk, row_chunk):
    bf = jnp.bfloat16
    n_a = ma_ref.shape[1]
    n_ff = D_FF // ff_chunk

    def head(rs):
        x1 = (x_ref[rs, :]
              + jnp.dot(ma_ref[rs, :], wo_ref[0:n_a, :], preferred_element_type=jnp.float32)
              + jnp.dot(mb_ref[rs, :], wo_ref[n_a:, :], preferred_element_type=jnp.float32))
        return x1, _rms(x1, g1_ref[...]).astype(bf)

    def ff(h, y, c):
        sl = slice(c * ff_chunk, (c + 1) * ff_chunk)
        a = jnp.dot(h, w1_ref[:, sl], preferred_element_type=jnp.float32)
        a = jnp.square(jnp.maximum(a, 0.0)).astype(bf)
        d = jnp.dot(a, w2_ref[sl, :], preferred_element_type=jnp.float32)
        return d if y is None else y + d

    chunks = [slice(r * row_chunk, (r + 1) * row_chunk) for r in range(x_ref.shape[0] // row_chunk)]
    x1, h = head(chunks[0])
    for r, rs in enumerate(chunks):
        y = None if r == 0 else y_first
        for c in range(0 if r == 0 else 1, n_ff - 1):
            y = ff(h, y, c)
        if r + 1 < len(chunks):
            x1_n, h_n = head(chunks[r + 1])
        y = ff(h, y, n_ff - 1)
        if r + 1 < len(chunks):
            y_first = ff(h_n, None, 0)
        o_ref[rs, :] = _rms(x1 + y, g2_ref[...])
        if r + 1 < len(chunks):
            x1, h = x1_n, h_n


def _mlp(x2d, mix_a, mix_b, w_out, g1, w1, w2, g2, tile, ff_chunk, row_chunk):
    N, D = x2d.shape
    const = lambda i: (0, 0)
    single = pl.Buffered(1)
    return pl.pallas_call(
        functools.partial(_mlp_kernel, ff_chunk=ff_chunk, row_chunk=row_chunk),
        grid=(N // tile,),
        in_specs=[
            pl.BlockSpec((tile, D), lambda i: (i, 0)),
            pl.BlockSpec((tile, mix_a.shape[1]), lambda i: (i, 0)),
            pl.BlockSpec((tile, mix_b.shape[1]), lambda i: (i, 0)),
            pl.BlockSpec(w_out.shape, const, pipeline_mode=single),
            pl.BlockSpec(g1.shape, const),
            pl.BlockSpec(w1.shape, const, pipeline_mode=single),
            pl.BlockSpec(w2.shape, const, pipeline_mode=single),
            pl.BlockSpec(g2.shape, const),
        ],
        out_specs=pl.BlockSpec((tile, D), lambda i: (i, 0)),
        out_shape=jax.ShapeDtypeStruct((N, D), jnp.float32),
        compiler_params=pltpu.CompilerParams(
            dimension_semantics=("arbitrary",), vmem_limit_bytes=VMEM_LIMIT),
        name="mlp",
    )(x2d, mix_a, mix_b, w_out, g1, w1, w2, g2)


def _swap_halves(w):
    half = w.shape[-1] // 2
    return jnp.concatenate([w[..., half:], w[..., :half]], axis=-1)


def kernel(x, positions, rel_bias, norm_attn, w_in, diff_lq1, diff_lk1, diff_lq2, diff_lk2,
           diff_subln, mla_q_norm, mla_w_uq, mla_kv_norm, mla_w_ukv, w_out, norm_mlp,
           w_mlp_in, w_mlp_out, norm_final):
    B, S, D = x.shape
    bf = jnp.bfloat16
    depth = w_in.shape[0]
    assert depth == 1
    l = 0
    row = lambda a: a.reshape(1, -1).astype(jnp.float32)

    w_in_l = w_in[l].astype(bf)
    n_diff = 2 * DIFF_QK_COLS + DIFF_V_COLS
    k_pe_cols = w_in_l[:, -MLA_ROPE_DIM:]
    w_in_x = jnp.concatenate([w_in_l[:, n_diff:], _swap_halves(k_pe_cols), w_in_l[:, :n_diff]],
                             axis=1)
    w_uq = mla_w_uq[l].astype(bf).reshape(MLA_Q_RANK, MLA_HEADS, MLA_NOPE_DIM + MLA_ROPE_DIM)
    q_pe_cols = w_uq[..., MLA_NOPE_DIM:]
    w_uq_x = jnp.concatenate([w_uq, _swap_halves(q_pe_cols)], axis=-1)
    w_uq_x = w_uq_x.reshape(MLA_Q_RANK, MLA_HEADS * MLA_QK_PAD)
    w_ukv = mla_w_ukv[l].astype(bf)

    tab = _rope_table(positions, 512)
    dq, dk, dv, mq, mk, mv = _proj(x, row(norm_attn[l]), w_in_x, row(mla_q_norm[l]), w_uq_x,
                                   row(mla_kv_norm[l]), w_ukv, tab, 512)
    mix_a = _attention(dq, dk, dv, positions, 256, 512, 4,
                       (rel_bias, diff_lq1[l], diff_lk1[l], diff_lq2[l], diff_lk2[l], diff_subln[l]))
    mix_b = _attention(mq, mk, mv, positions, 512, 512, 4)
    out = _mlp(x.reshape(B * S, D), mix_a.reshape(B * S, -1), mix_b.reshape(B * S, -1),
               w_out[l].astype(bf), row(norm_mlp[l]), w_mlp_in[l].astype(bf),
               w_mlp_out[l].astype(bf), row(norm_final), 1024, 512, 512)
    return out.reshape(B, S, D)
```

```python
import functools
import math

import jax
import jax.numpy as jnp
from jax import lax
from jax.experimental import pallas as pl
from jax.experimental.pallas import tpu as pltpu

D_MODEL = 1024
DIFF_HEADS = 4
DIFF_HEAD_DIM = 64
DIFF_V_DIM = 128
DIFF_QK_COLS = DIFF_HEADS * 2 * DIFF_HEAD_DIM
DIFF_V_COLS = DIFF_HEADS * DIFF_V_DIM
MLA_HEADS = 4
MLA_Q_RANK = 384
MLA_KV_RANK = 256
MLA_NOPE_DIM = 128
MLA_ROPE_DIM = 64
MLA_V_DIM = 128
MLA_QK_PAD = 256
ROPE_BASE = 10000.0
D_FF = 4 * D_MODEL
REL_BUCKETS = 32
REL_MAX_DIST = 128
NORM_EPS = 1e-6
NEG_INF = -1e30
LOG2E = math.log2(math.e)
LAMBDA_INIT = 0.8 - 0.6 * math.exp(-0.3 * 0)

LANES = 128
VMEM_LIMIT = 56 * 1024 * 1024

NT_DIMS = (((1,), (1,)), ((), ()))


def _rms(x, gain):
    return x * lax.rsqrt(jnp.mean(x * x, axis=-1, keepdims=True) + NORM_EPS) * gain


def _rope_table_kernel(pos_ref, tab_ref):
    half = MLA_ROPE_DIM // 2
    lane = lax.broadcasted_iota(jnp.int32, (1, LANES), 1)
    fidx = (lane % half).astype(jnp.float32) * 2.0
    inv_freq = jnp.exp(-(fidx / MLA_ROPE_DIM) * math.log(ROPE_BASE))
    ang = pos_ref[...].astype(jnp.float32) * inv_freq
    c, s = jnp.cos(ang), jnp.sin(ang)
    tab_ref[...] = jnp.where(lane < 2 * half, c, jnp.where(lane < 3 * half, -s, s))


def _rope_table(positions, tile):
    seq = positions.shape[0]
    return pl.pallas_call(
        _rope_table_kernel,
        grid=(seq // tile,),
        in_specs=[pl.BlockSpec((tile, 1), lambda i: (i, 0))],
        out_specs=pl.BlockSpec((tile, LANES), lambda i: (i, 0)),
        out_shape=jax.ShapeDtypeStruct((seq, LANES), jnp.float32),
        name="rope_table",
    )(positions.reshape(seq, 1))


def _rope_slab(slab, tab):
    prod = slab * tab
    return prod + pltpu.roll(prod, MLA_ROPE_DIM, axis=1)


def _proj_kernel(x_ref, g_ref, w_in_ref, qn_ref, w_uq_ref, kvn_ref, w_ukv_ref, tab_ref,
                 dq_ref, dk_ref, dv_ref, mq_ref, mk_ref, mv_ref):
    bf = jnp.bfloat16
    x = x_ref[0]
    h = (x * g_ref[...]).astype(bf)
    r = lax.rsqrt(jnp.mean(x * x, axis=-1, keepdims=True) + NORM_EPS)
    o_ckv = MLA_Q_RANK
    o_kpe = o_ckv + MLA_KV_RANK
    o_dq = o_kpe + LANES
    o_dk = o_dq + DIFF_QK_COLS
    o_dv = o_dk + DIFF_QK_COLS
    p = jnp.dot(h, w_in_ref[:, :o_dq], preferred_element_type=jnp.float32) * r
    tab = tab_ref[...]
    lane = lax.broadcasted_iota(jnp.int32, tab.shape, 1)

    dscale = (DIFF_HEAD_DIM ** -0.5) * LOG2E
    pq = jnp.dot(h, w_in_ref[:, o_dq:o_dk], preferred_element_type=jnp.float32)
    rq = r * dscale
    for hh in range(DIFF_HEADS):
        dq_ref[0, hh] = (pq[:, hh * LANES:(hh + 1) * LANES] * rq).astype(bf)

    cq = _rms(p[:, :o_ckv], qn_ref[...]).astype(bf)
    q = jnp.dot(cq, w_uq_ref[...], preferred_element_type=jnp.float32)
    ckv = _rms(p[:, o_ckv:o_kpe], kvn_ref[...]).astype(bf)
    kv = jnp.dot(ckv, w_ukv_ref[...], preferred_element_type=jnp.float32)
    k_rope = _rope_slab(p[:, o_kpe:o_dq], tab).astype(bf)

    pkv = jnp.dot(h, w_in_ref[:, o_dk:], preferred_element_type=jnp.float32)
    for hh in range(DIFF_HEADS):
        dk_ref[0, hh] = (pkv[:, hh * LANES:(hh + 1) * LANES] * r).astype(bf)
        dv_ref[0, hh] = (pkv[:, DIFF_QK_COLS + hh * LANES:DIFF_QK_COLS + (hh + 1) * LANES]
                         * r).astype(bf)

    mscale = ((MLA_NOPE_DIM + MLA_ROPE_DIM) ** -0.5) * LOG2E
    for hh in range(MLA_HEADS):
        base = hh * MLA_QK_PAD
        q_rope = _rope_slab(q[:, base + LANES:base + 2 * LANES], tab)
        q_rope = jnp.where(lane < MLA_ROPE_DIM, q_rope, 0.0)
        mq_ref[0, hh, :, 0:LANES] = (q[:, base:base + LANES] * mscale).astype(bf)
        mq_ref[0, hh, :, LANES:2 * LANES] = (q_rope * mscale).astype(bf)
        mk_ref[0, hh, :, 0:LANES] = kv[:, base:base + LANES].astype(bf)
        mk_ref[0, hh, :, LANES:2 * LANES] = k_rope
        mv_ref[0, hh] = kv[:, base + LANES:base + 2 * LANES].astype(bf)


def _proj(x, g, w_in, qn, w_uq, kvn, w_ukv, tab, tile):
    B, S, D = x.shape
    bf = jnp.bfloat16
    const = lambda b, i: (0, 0)
    head_out = lambda w: pl.BlockSpec((1, DIFF_HEADS, tile, w), lambda b, i: (b, 0, i, 0))
    shp = lambda w: jax.ShapeDtypeStruct((B, DIFF_HEADS, S, w), bf)
    return pl.pallas_call(
        _proj_kernel,
        grid=(B, S // tile),
        in_specs=[
            pl.BlockSpec((1, tile, D), lambda b, i: (b, i, 0)),
            pl.BlockSpec(g.shape, const),
            pl.BlockSpec(w_in.shape, const),
            pl.BlockSpec(qn.shape, const),
            pl.BlockSpec(w_uq.shape, const),
            pl.BlockSpec(kvn.shape, const),
            pl.BlockSpec(w_ukv.shape, const),
            pl.BlockSpec((tile, LANES), lambda b, i: (i, 0)),
        ],
        out_specs=[head_out(LANES), head_out(LANES), head_out(LANES),
                   head_out(MLA_QK_PAD), head_out(MLA_QK_PAD), head_out(LANES)],
        out_shape=[shp(LANES), shp(LANES), shp(LANES),
                   shp(MLA_QK_PAD), shp(MLA_QK_PAD), shp(LANES)],
        compiler_params=pltpu.CompilerParams(
            dimension_semantics=("arbitrary", "arbitrary"), vmem_limit_bytes=VMEM_LIMIT),
        name="proj",
    )(x, g, w_in, qn, w_uq, kvn, w_ukv, tab)


def _schedule(positions, tq, tk, far):
    seq = positions.shape[0]
    qp = positions.reshape(seq // tq, tq)
    kp = positions.reshape(seq // tk, tk)
    qmin, qmax = qp.min(axis=1), qp.max(axis=1)
    kmin, kmax = kp.min(axis=1), kp.max(axis=1)
    skip = kmin[None, :] > qmax[:, None]
    interior = (qmin[:, None] - kmax[None, :]) >= far
    cls = jnp.where(interior, 0, jnp.where(skip, 2, 1)).astype(jnp.int32)
    order = jnp.argsort(cls, axis=1, stable=True).astype(jnp.int32)
    n_int = jnp.sum(cls == 0, axis=1).astype(jnp.int32)
    n_vis = jnp.sum(cls < 2, axis=1).astype(jnp.int32)
    return order.reshape(-1), n_int, n_vis


def _t5_bias_row(relb_ref, head):
    n = lax.broadcasted_iota(jnp.int32, (1, LANES), 1)
    max_exact = REL_BUCKETS // 2
    nf = jnp.maximum(n, 1).astype(jnp.float32)
    large = max_exact + jnp.floor(jnp.log(nf / max_exact) / math.log(REL_MAX_DIST / max_exact)
                                  * (REL_BUCKETS - max_exact)).astype(jnp.int32)
    large = jnp.minimum(large, REL_BUCKETS - 1)
    bucket = jnp.where(n < max_exact, n, large)
    row = jnp.zeros((1, LANES), jnp.float32)
    for b in range(REL_BUCKETS):
        row = jnp.where(bucket == b, relb_ref[b, head] * LOG2E, row)
    return row


def _attn_kernel(*refs, tq, tk, nk, hp, diff):
    if diff:
        (tbl_ref, nint_ref, nvis_ref, relb_ref, q_ref, k_ref, v_ref, qn_ref, kn_ref,
         qpos_ref, kpos_ref, lq1_ref, lk1_ref, lq2_ref, lk2_ref, subln_ref,
         o_ref, m_ref, acc_ref, s0_ref, s1_ref, add_ref) = refs
        grp, i, batch = pl.program_id(0), pl.program_id(1), pl.program_id(2)
        n_vis = nvis_ref[i]
        first = batch == 0
        start = (batch * n_vis) % 2
    else:
        (tbl_ref, nint_ref, nvis_ref, start_ref, q_ref, k_ref, v_ref, qn_ref, qpos_ref, kpos_ref,
         o_ref, m_ref, acc_ref, s0_ref, s1_ref) = refs
        grp, batch, i = pl.program_id(0), pl.program_id(1), pl.program_id(2)
        n_vis = nvis_ref[i]
        first = i == 0
        start = start_ref[i]
        i_next = jnp.minimum(i + 1, pl.num_programs(2) - 1)
    s_refs = (s0_ref, s1_ref)
    n_int = jnp.minimum(nint_ref[i], n_vis - 1)
    nc = tk // LANES
    bf = jnp.bfloat16

    def _build_add_tiles():
        qpos = qpos_ref[...]
        tabs = [jnp.broadcast_to(_t5_bias_row(relb_ref, grp * hp + hh), (tq, LANES))
                for hh in range(hp)]

        def fill(slot, carry):
            kpos = kpos_ref[tbl_ref[i * nk + n_int + slot]]
            for c in range(nc):
                cs = slice(c * LANES, (c + 1) * LANES)
                d = qpos - kpos[:, cs]
                idx = jnp.clip(d, 0, LANES - 1)
                for hh in range(hp):
                    add_ref[slot, hh, :, cs] = jnp.where(
                        d >= 0, jnp.take_along_axis(tabs[hh], idx, axis=1), NEG_INF)
            return carry

        lax.fori_loop(0, n_vis - n_int, fill, 0)

    if diff:
        pl.when(first)(_build_add_tiles)

    def q_rows(ref, hh):
        q = ref[0, hh]
        if not diff:
            return q
        lane = lax.broadcasted_iota(jnp.int32, (tq, LANES), 1)
        zero = jnp.zeros_like(q)
        return jnp.concatenate([jnp.where(lane < DIFF_HEAD_DIM, q, zero),
                                jnp.where(lane >= DIFF_HEAD_DIM, q, zero)], axis=0)

    if diff:
        far_bias = [relb_ref[REL_BUCKETS - 1, grp * hp + hh] * LOG2E for hh in range(hp)]

    m_ref[...] = jnp.full(m_ref.shape, NEG_INF, jnp.float32)
    acc_ref[...] = jnp.zeros(acc_ref.shape, jnp.float32)

    def k_off(t):
        return pl.multiple_of(tbl_ref[i * nk + t] * tk, tk)

    mxu_n = 2 * LANES
    n_sub = tk // mxu_n

    def softmax(t, s, hh, interior):
        m_old = m_ref[hh]
        if interior:
            m_cur = jnp.max(s, axis=1, keepdims=True)
            if diff:
                m_new = jnp.maximum(m_old, m_cur + far_bias[hh])
                shift = m_new - far_bias[hh]
            else:
                m_new = jnp.maximum(m_old, m_cur)
                shift = m_new
        else:
            if diff:
                add = add_ref[t - n_int, hh]
                s = s + jnp.concatenate([add, add], axis=0)
            else:
                qpos = qpos_ref[...]
                kpos = kpos_ref[tbl_ref[i * nk + t]]
                s = jnp.concatenate(
                    [jnp.where(qpos >= kpos[:, c * LANES:(c + 1) * LANES],
                               s[:, c * LANES:(c + 1) * LANES], NEG_INF) for c in range(nc)], axis=1)
            m_new = jnp.maximum(m_old, jnp.max(s, axis=1, keepdims=True))
            shift = m_new
        m_ref[hh] = m_new
        p = jnp.concatenate([jnp.exp2(s[:, c * LANES:(c + 1) * LANES] - shift)
                             for c in range(nc)], axis=1).astype(bf)
        return p, jnp.exp2(m_old - m_new)

    def visit(t_sm, s_cur, interior, t_sc, s_nxt):
        nxt = isinstance(t_sc, str)
        off_sm = None if t_sm is None else k_off(t_sm)
        if t_sc is None or (nxt and diff):
            off_sc = None
        elif nxt:
            off_sc = pl.multiple_of(tbl_ref[i_next * nk] * tk, tk)
        else:
            off_sc = k_off(t_sc)
        sm, pvs, qrows = {}, {}, {}

        def score_dot(hh, c):
            cs = slice(c * mxu_n, (c + 1) * mxu_n)
            if hh not in qrows:
                qrows[hh] = q_rows(qn_ref if nxt else q_ref, hh)
            q = qrows[hh]
            if nxt and diff:
                k = kn_ref[0, hh, cs, :]
            else:
                k = k_ref[0, hh, pl.ds(off_sc + c * mxu_n, mxu_n), :]
            s_nxt[hh, :, cs] = lax.dot_general(q, k, NT_DIMS, preferred_element_type=jnp.float32)

        def pv_dot(hh, c):
            cs = slice(c * mxu_n, (c + 1) * mxu_n)
            v = v_ref[0, hh, pl.ds(off_sm + c * mxu_n, mxu_n), :]
            d = jnp.dot(sm[hh][0][:, cs], jnp.concatenate([v, jnp.ones_like(v)], axis=1),
                        preferred_element_type=jnp.float32)
            pvs[hh] = d if hh not in pvs else pvs[hh] + d

        tiles = [(hh, c) for hh in range(hp) for c in range(n_sub)]
        for hh, c in tiles:
            if t_sm is not None and hh not in sm:
                sm[hh] = softmax(t_sm, s_cur[hh], hh, interior)
            if t_sc is not None:
                score_dot(hh, c)
            if t_sm is not None:
                pv_dot(hh, c)
                if c == n_sub - 1:
                    alpha = sm[hh][1]
                    acc_ref[hh] = jnp.concatenate([alpha, alpha], axis=1) * acc_ref[hh] + pvs[hh]

    def step(interior):
        def body(t, carry):
            for par in (0, 1):
                @pl.when((start + t) % 2 == par)
                def _():
                    visit(t, s_refs[par], interior, t + 1, s_refs[1 - par])
            return carry
        return body

    @pl.when(first)
    def _first_scores():
        visit(None, None, False, 0, s_refs[0])

    lax.fori_loop(0, n_int, step(True), 0)
    lax.fori_loop(n_int, n_vis - 1, step(False), 0)
    for par in (0, 1):
        @pl.when((start + n_vis - 1) % 2 == par)
        def _():
            visit(n_vis - 1, s_refs[par], False, "next", s_refs[1 - par])

    if diff:
        lam = (jnp.exp(jnp.sum(lq1_ref[...] * lk1_ref[...], axis=1, keepdims=True))
               - jnp.exp(jnp.sum(lq2_ref[...] * lk2_ref[...], axis=1, keepdims=True))
               + LAMBDA_INIT)
    for hh in range(hp):
        acc = acc_ref[hh]
        o = acc[:, :LANES] / acc[:, LANES:]
        if diff:
            oo = o[:tq] - lam * o[tq:]
            o = _rms(oo, subln_ref[...]) * (1.0 - LAMBDA_INIT)
        o_ref[0, :, hh * LANES:(hh + 1) * LANES] = o.astype(o_ref.dtype)


def _lane_rep(positions):
    return jnp.broadcast_to(positions[:, None], (positions.shape[0], LANES))


def _attention(q, k, v, positions, tq, tk, hp, diff_params=None):
    B, H, S, width = q.shape
    diff = diff_params is not None
    nq, nk = S // tq, S // tk
    rows = 2 * tq if diff else tq
    tbl, n_int, n_vis = _schedule(positions, tq, tk, LANES if diff else 0)
    imap = ((lambda f: (lambda g, i, b, *_: f(g, i, b))) if diff
            else (lambda f: (lambda g, b, i, *_: f(g, i, b))))
    const = imap(lambda g, i, b: (0, 0))
    nb = lambda b: jnp.minimum(b + 1, B - 1)
    ni = lambda i: jnp.minimum(i + 1, nq - 1)
    in_specs = [
        pl.BlockSpec((1, hp, tq, width), imap(lambda g, i, b: (b, g, i, 0))),
        pl.BlockSpec((1, hp, S, width), imap(lambda g, i, b: (b, g, 0, 0))),
        pl.BlockSpec((1, hp, S, LANES), imap(lambda g, i, b: (b, g, 0, 0))),
    ]
    operands = [q, k, v, q]
    scratch = [pltpu.VMEM((hp, rows, LANES), jnp.float32),
               pltpu.VMEM((hp, rows, 2 * LANES), jnp.float32),
               pltpu.VMEM((hp, rows, tk), jnp.float32),
               pltpu.VMEM((hp, rows, tk), jnp.float32)]
    if diff:
        in_specs += [
            pl.BlockSpec((1, hp, tq, width), imap(lambda g, i, b: (nb(b), g, i, 0))),
            pl.BlockSpec((1, hp, tk, width), lambda g, i, b, tbl, *_: (nb(b), g, tbl[i * nk], 0)),
        ]
        operands.append(k)
        scratch.append(pltpu.VMEM((nk, hp, tq, tk), jnp.float32))
        prefetch = [tbl, n_int, n_vis]
    else:
        in_specs.append(pl.BlockSpec((1, hp, tq, width), imap(lambda g, i, b: (b, g, ni(i), 0))))
        start = ((jnp.cumsum(n_vis) - n_vis) % 2).astype(jnp.int32)
        prefetch = [tbl, n_int, n_vis, start]
    in_specs += [
        pl.BlockSpec((tq, LANES), imap(lambda g, i, b: (i, 0))),
        pl.BlockSpec((nk, 1, tk), imap(lambda g, i, b: (0, 0, 0))),
    ]
    operands += [_lane_rep(positions), positions.reshape(nk, 1, tk)]
    if diff:
        rel_bias, lq1, lk1, lq2, lk2, subln = diff_params
        row = lambda a: a.reshape(1, -1).astype(jnp.float32)
        prefetch.append(rel_bias.astype(jnp.float32))
        operands += [row(lq1), row(lk1), row(lq2), row(lk2), row(subln)]
        in_specs += [pl.BlockSpec((1, DIFF_HEAD_DIM), const)] * 4 + [pl.BlockSpec((1, DIFF_V_DIM), const)]
    grid_spec = pltpu.PrefetchScalarGridSpec(
        num_scalar_prefetch=len(prefetch),
        grid=(H // hp, nq, B) if diff else (H // hp, B, nq),
        in_specs=in_specs,
        out_specs=pl.BlockSpec((1, tq, hp * LANES), imap(lambda g, i, b: (b, i, g))),
        scratch_shapes=scratch,
    )
    return pl.pallas_call(
        functools.partial(_attn_kernel, tq=tq, tk=tk, nk=nk, hp=hp, diff=diff),
        grid_spec=grid_spec,
        out_shape=jax.ShapeDtypeStruct((B, S, H * LANES), jnp.bfloat16),
        compiler_params=pltpu.CompilerParams(
            dimension_semantics=("arbitrary",) * 3, vmem_limit_bytes=VMEM_LIMIT),
        name="diff_attn" if diff else "mla_attn",
    )(*prefetch, *operands)


def _mlp_kernel(x_ref, ma_ref, mb_ref, wo_ref, g1_ref, w1_ref, w2_ref, g2_ref, o_ref, *,
                ff_chunk, row_chunk):
    bf = jnp.bfloat16
    n_a = ma_ref.shape[1]
    n_ff = D_FF // ff_chunk

    def head(rs):
        x1 = (x_ref[rs, :]
              + jnp.dot(ma_ref[rs, :], wo_ref[0:n_a, :], preferred_element_type=jnp.float32)
              + jnp.dot(mb_ref[rs, :], wo_ref[n_a:, :], preferred_element_type=jnp.float32))
        return x1, _rms(x1, g1_ref[...]).astype(bf)

    def ff(h, y, c):
        sl = slice(c * ff_chunk, (c + 1) * ff_chunk)
        a = jnp.dot(h, w1_ref[:, sl], preferred_element_type=jnp.float32)
        a = jnp.square(jnp.maximum(a, 0.0)).astype(bf)
        d = jnp.dot(a, w2_ref[sl, :], preferred_element_type=jnp.float32)
        return d if y is None else y + d

    chunks = [slice(r * row_chunk, (r + 1) * row_chunk) for r in range(x_ref.shape[0] // row_chunk)]
    x1, h = head(chunks[0])
    for r, rs in enumerate(chunks):
        y = None if r == 0 else y_first
        for c in range(0 if r == 0 else 1, n_ff - 1):
            y = ff(h, y, c)
        if r + 1 < len(chunks):
            x1_n, h_n = head(chunks[r + 1])
        y = ff(h, y, n_ff - 1)
        if r + 1 < len(chunks):
            y_first = ff(h_n, None, 0)
        o_ref[rs, :] = _rms(x1 + y, g2_ref[...])
        if r + 1 < len(chunks):
            x1, h = x1_n, h_n


def _mlp(x2d, mix_a, mix_b, w_out, g1, w1, w2, g2, tile, ff_chunk, row_chunk):
    N, D = x2d.shape
    const = lambda i: (0, 0)
    single = pl.Buffered(1)
    return pl.pallas_call(
        functools.partial(_mlp_kernel, ff_chunk=ff_chunk, row_chunk=row_chunk),
        grid=(N // tile,),
        in_specs=[
            pl.BlockSpec((tile, D), lambda i: (i, 0)),
            pl.BlockSpec((tile, mix_a.shape[1]), lambda i: (i, 0)),
            pl.BlockSpec((tile, mix_b.shape[1]), lambda i: (i, 0)),
            pl.BlockSpec(w_out.shape, const, pipeline_mode=single),
            pl.BlockSpec(g1.shape, const),
            pl.BlockSpec(w1.shape, const, pipeline_mode=single),
            pl.BlockSpec(w2.shape, const, pipeline_mode=single),
            pl.BlockSpec(g2.shape, const),
        ],
        out_specs=pl.BlockSpec((tile, D), lambda i: (i, 0)),
        out_shape=jax.ShapeDtypeStruct((N, D), jnp.float32),
        compiler_params=pltpu.CompilerParams(
            dimension_semantics=("arbitrary",), vmem_limit_bytes=VMEM_LIMIT),
        name="mlp",
    )(x2d, mix_a, mix_b, w_out, g1, w1, w2, g2)


def _swap_halves(w):
    half = w.shape[-1] // 2
    return jnp.concatenate([w[..., half:], w[..., :half]], axis=-1)


def kernel(x, positions, rel_bias, norm_attn, w_in, diff_lq1, diff_lk1, diff_lq2, diff_lk2,
           diff_subln, mla_q_norm, mla_w_uq, mla_kv_norm, mla_w_ukv, w_out, norm_mlp,
           w_mlp_in, w_mlp_out, norm_final):
    B, S, D = x.shape
    bf = jnp.bfloat16
    depth = w_in.shape[0]
    assert depth == 1
    l = 0
    row = lambda a: a.reshape(1, -1).astype(jnp.float32)

    w_in_l = w_in[l].astype(bf)
    n_diff = 2 * DIFF_QK_COLS + DIFF_V_COLS
    k_pe_cols = w_in_l[:, -MLA_ROPE_DIM:]
    w_in_x = jnp.concatenate([w_in_l[:, n_diff:], _swap_halves(k_pe_cols), w_in_l[:, :n_diff]],
                             axis=1)
    w_uq = mla_w_uq[l].astype(bf).reshape(MLA_Q_RANK, MLA_HEADS, MLA_NOPE_DIM + MLA_ROPE_DIM)
    q_pe_cols = w_uq[..., MLA_NOPE_DIM:]
    w_uq_x = jnp.concatenate([w_uq, _swap_halves(q_pe_cols)], axis=-1)
    w_uq_x = w_uq_x.reshape(MLA_Q_RANK, MLA_HEADS * MLA_QK_PAD)
    w_ukv = mla_w_ukv[l].astype(bf)

    tab = _rope_table(positions, 512)
    dq, dk, dv, mq, mk, mv = _proj(x, row(norm_attn[l]), w_in_x, row(mla_q_norm[l]), w_uq_x,
                                   row(mla_kv_norm[l]), w_ukv, tab, 512)
    mix_a = _attention(dq, dk, dv, positions, 256, 512, 4,
                       (rel_bias, diff_lq1[l], diff_lk1[l], diff_lq2[l], diff_lk2[l], diff_subln[l]))
    mix_b = _attention(mq, mk, mv, positions, 512, 512, 4)
    out = _mlp(x.reshape(B * S, D), mix_a.reshape(B * S, -1), mix_b.reshape(B * S, -1),
               w_out[l].astype(bf), row(norm_mlp[l]), w_mlp_in[l].astype(bf),
               w_mlp_out[l].astype(bf), row(norm_final), 1024, 512, 512)
    return out.reshape(B, S, D)
```

```python
import functools
import math

import jax
import jax.numpy as jnp
from jax import lax
from jax.experimental import pallas as pl
from jax.experimental.pallas import tpu as pltpu

D_MODEL = 1024
DIFF_HEADS = 4
DIFF_HEAD_DIM = 64
DIFF_V_DIM = 128
DIFF_QK_COLS = DIFF_HEADS * 2 * DIFF_HEAD_DIM
DIFF_V_COLS = DIFF_HEADS * DIFF_V_DIM
MLA_HEADS = 4
MLA_Q_RANK = 384
MLA_KV_RANK = 256
MLA_NOPE_DIM = 128
MLA_ROPE_DIM = 64
MLA_V_DIM = 128
MLA_QK_PAD = 256
ROPE_BASE = 10000.0
D_FF = 4 * D_MODEL
REL_BUCKETS = 32
REL_MAX_DIST = 128
NORM_EPS = 1e-6
NEG_INF = -1e30
LOG2E = math.log2(math.e)
LAMBDA_INIT = 0.8 - 0.6 * math.exp(-0.3 * 0)

LANES = 128
VMEM_LIMIT = 56 * 1024 * 1024

ROPE_TILE = 512
PROJ_TILE = 512
DIFF_TQ, DIFF_TK, DIFF_HEADS_PER_STEP = 256, 512, 4
MLA_TQ, MLA_TK, MLA_HEADS_PER_STEP = 512, 512, 4
MLP_TILE, MLP_ROW_CHUNK, MLP_FF_CHUNK = 1024, 512, 512

NT_DIMS = (((1,), (1,)), ((), ()))


def _rms(x, gain):
    return x * lax.rsqrt(jnp.mean(x * x, axis=-1, keepdims=True) + NORM_EPS) * gain


def _rope_table_kernel(pos_ref, tab_ref):
    half = MLA_ROPE_DIM // 2
    lane = lax.broadcasted_iota(jnp.int32, (1, LANES), 1)
    fidx = (lane % half).astype(jnp.float32) * 2.0
    inv_freq = jnp.exp(-(fidx / MLA_ROPE_DIM) * math.log(ROPE_BASE))
    ang = pos_ref[...].astype(jnp.float32) * inv_freq
    c, s = jnp.cos(ang), jnp.sin(ang)
    tab_ref[...] = jnp.where(lane < 2 * half, c, jnp.where(lane < 3 * half, -s, s))


def _rope_table(positions, tile):
    seq = positions.shape[0]
    return pl.pallas_call(
        _rope_table_kernel,
        grid=(seq // tile,),
        in_specs=[pl.BlockSpec((tile, 1), lambda i: (i, 0))],
        out_specs=pl.BlockSpec((tile, LANES), lambda i: (i, 0)),
        out_shape=jax.ShapeDtypeStruct((seq, LANES), jnp.float32),
        name="rope_table",
    )(positions.reshape(seq, 1))


def _rope_slab(slab, tab):
    prod = slab * tab
    return prod + pltpu.roll(prod, MLA_ROPE_DIM, axis=1)


def _proj_kernel(x_ref, g_ref, w_in_ref, qn_ref, w_uq_ref, kvn_ref, w_ukv_ref, tab_ref,
                 dq_ref, dk_ref, dv_ref, mq_ref, mk_ref, mv_ref):
    bf = jnp.bfloat16
    x = x_ref[0]
    h = (x * g_ref[...]).astype(bf)
    r = lax.rsqrt(jnp.mean(x * x, axis=-1, keepdims=True) + NORM_EPS)
    o_ckv = MLA_Q_RANK
    o_kpe = o_ckv + MLA_KV_RANK
    o_dq = o_kpe + LANES
    o_dk = o_dq + DIFF_QK_COLS
    o_dv = o_dk + DIFF_QK_COLS
    p = jnp.dot(h, w_in_ref[:, :o_dq], preferred_element_type=jnp.float32) * r
    tab = tab_ref[...]
    lane = lax.broadcasted_iota(jnp.int32, tab.shape, 1)

    dscale = (DIFF_HEAD_DIM ** -0.5) * LOG2E
    pq = jnp.dot(h, w_in_ref[:, o_dq:o_dk], preferred_element_type=jnp.float32)
    rq = r * dscale
    for hh in range(DIFF_HEADS):
        dq_ref[0, hh] = (pq[:, hh * LANES:(hh + 1) * LANES] * rq).astype(bf)

    cq = _rms(p[:, :o_ckv], qn_ref[...]).astype(bf)
    q = jnp.dot(cq, w_uq_ref[...], preferred_element_type=jnp.float32)
    ckv = _rms(p[:, o_ckv:o_kpe], kvn_ref[...]).astype(bf)
    kv = jnp.dot(ckv, w_ukv_ref[...], preferred_element_type=jnp.float32)
    k_rope = _rope_slab(p[:, o_kpe:o_dq], tab).astype(bf)

    pkv = jnp.dot(h, w_in_ref[:, o_dk:], preferred_element_type=jnp.float32)
    for hh in range(DIFF_HEADS):
        dk_ref[0, hh] = (pkv[:, hh * LANES:(hh + 1) * LANES] * r).astype(bf)
        dv_ref[0, hh] = (pkv[:, DIFF_QK_COLS + hh * LANES:DIFF_QK_COLS + (hh + 1) * LANES]
                         * r).astype(bf)

    mscale = ((MLA_NOPE_DIM + MLA_ROPE_DIM) ** -0.5) * LOG2E
    for hh in range(MLA_HEADS):
        base = hh * MLA_QK_PAD
        q_rope = _rope_slab(q[:, base + LANES:base + 2 * LANES], tab)
        q_rope = jnp.where(lane < MLA_ROPE_DIM, q_rope, 0.0)
        mq_ref[0, hh, :, 0:LANES] = (q[:, base:base + LANES] * mscale).astype(bf)
        mq_ref[0, hh, :, LANES:2 * LANES] = (q_rope * mscale).astype(bf)
        mk_ref[0, hh, :, 0:LANES] = kv[:, base:base + LANES].astype(bf)
        mk_ref[0, hh, :, LANES:2 * LANES] = k_rope
        mv_ref[0, hh] = kv[:, base + LANES:base + 2 * LANES].astype(bf)


def _proj(x, g, w_in, qn, w_uq, kvn, w_ukv, tab, tile):
    B, S, D = x.shape
    bf = jnp.bfloat16
    const = lambda b, i: (0, 0)
    head_out = lambda w: pl.BlockSpec((1, DIFF_HEADS, tile, w), lambda b, i: (b, 0, i, 0))
    shp = lambda w: jax.ShapeDtypeStruct((B, DIFF_HEADS, S, w), bf)
    return pl.pallas_call(
        _proj_kernel,
        grid=(B, S // tile),
        in_specs=[
            pl.BlockSpec((1, tile, D), lambda b, i: (b, i, 0)),
            pl.BlockSpec(g.shape, const),
            pl.BlockSpec(w_in.shape, const),
            pl.BlockSpec(qn.shape, const),
            pl.BlockSpec(w_uq.shape, const),
            pl.BlockSpec(kvn.shape, const),
            pl.BlockSpec(w_ukv.shape, const),
            pl.BlockSpec((tile, LANES), lambda b, i: (i, 0)),
        ],
        out_specs=[head_out(LANES), head_out(LANES), head_out(LANES),
                   head_out(MLA_QK_PAD), head_out(MLA_QK_PAD), head_out(LANES)],
        out_shape=[shp(LANES), shp(LANES), shp(LANES),
                   shp(MLA_QK_PAD), shp(MLA_QK_PAD), shp(LANES)],
        compiler_params=pltpu.CompilerParams(
            dimension_semantics=("arbitrary", "arbitrary"), vmem_limit_bytes=VMEM_LIMIT),
        name="proj",
    )(x, g, w_in, qn, w_uq, kvn, w_ukv, tab)


def _schedule(positions, tq, tk, far):
    seq = positions.shape[0]
    qp = positions.reshape(seq // tq, tq)
    kp = positions.reshape(seq // tk, tk)
    qmin, qmax = qp.min(axis=1), qp.max(axis=1)
    kmin, kmax = kp.min(axis=1), kp.max(axis=1)
    skip = kmin[None, :] > qmax[:, None]
    interior = (qmin[:, None] - kmax[None, :]) >= far
    cls = jnp.where(interior, 0, jnp.where(skip, 2, 1)).astype(jnp.int32)
    order = jnp.argsort(cls, axis=1, stable=True).astype(jnp.int32)
    n_int = jnp.sum(cls == 0, axis=1).astype(jnp.int32)
    n_vis = jnp.sum(cls < 2, axis=1).astype(jnp.int32)
    return order.reshape(-1), n_int, n_vis


def _t5_bias_row(relb_ref, head):
    n = lax.broadcasted_iota(jnp.int32, (1, LANES), 1)
    max_exact = REL_BUCKETS // 2
    nf = jnp.maximum(n, 1).astype(jnp.float32)
    large = max_exact + jnp.floor(jnp.log(nf / max_exact) / math.log(REL_MAX_DIST / max_exact)
                                  * (REL_BUCKETS - max_exact)).astype(jnp.int32)
    large = jnp.minimum(large, REL_BUCKETS - 1)
    bucket = jnp.where(n < max_exact, n, large)
    row = jnp.zeros((1, LANES), jnp.float32)
    for b in range(REL_BUCKETS):
        row = jnp.where(bucket == b, relb_ref[b, head] * LOG2E, row)
    return row


def _attn_kernel(*refs, tq, tk, nk, hp, diff):
    if diff:
        (tbl_ref, nint_ref, nvis_ref, qmin_ref, qmax_ref, kbmin_ref, kbmax_ref, relb_ref,
         q_ref, k_ref, v_ref, qn_ref, kn_ref,
         qpos_ref, kpos_ref, lq1_ref, lk1_ref, lq2_ref, lk2_ref, subln_ref,
         o_ref, m_ref, acc_ref, s0_ref, s1_ref, add_ref) = refs
        grp, i, batch = pl.program_id(0), pl.program_id(1), pl.program_id(2)
        n_vis = nvis_ref[i]
        first = batch == 0
        start = (batch * n_vis) % 2
    else:
        (tbl_ref, nint_ref, nvis_ref, start_ref, q_ref, k_ref, v_ref, qn_ref, qpos_ref, kpos_ref,
         o_ref, m_ref, acc_ref, s0_ref, s1_ref) = refs
        grp, batch, i = pl.program_id(0), pl.program_id(1), pl.program_id(2)
        n_vis = nvis_ref[i]
        first = i == 0
        start = start_ref[i]
        i_next = jnp.minimum(i + 1, pl.num_programs(2) - 1)
    s_refs = (s0_ref, s1_ref)
    n_int = jnp.minimum(nint_ref[i], n_vis - 1)
    nc = tk // LANES
    bf = jnp.bfloat16

    if diff:
        far_bias = [relb_ref[REL_BUCKETS - 1, grp * hp + hh] * LOG2E for hh in range(hp)]

    def _build_add_tiles():
        qpos = qpos_ref[...]
        tabs = [jnp.broadcast_to(_t5_bias_row(relb_ref, grp * hp + hh), (tq, LANES))
                for hh in range(hp)]

        def fill(slot, carry):
            j = tbl_ref[i * nk + n_int + slot]
            kpos = kpos_ref[j]
            for c in range(nc):
                cs = slice(c * LANES, (c + 1) * LANES)
                blk = j * nc + c
                far = qmin_ref[i] - kbmax_ref[blk] >= LANES - 1
                dead = kbmin_ref[blk] > qmax_ref[i]

                @pl.when(far)
                def _():
                    for hh in range(hp):
                        add_ref[slot, hh, :, cs] = jnp.full((tq, LANES), far_bias[hh], jnp.float32)

                @pl.when(dead)
                def _():
                    for hh in range(hp):
                        add_ref[slot, hh, :, cs] = jnp.full((tq, LANES), NEG_INF, jnp.float32)

                @pl.when(jnp.logical_not(jnp.logical_or(far, dead)))
                def _():
                    d = qpos - kpos[:, cs]
                    idx = jnp.clip(d, 0, LANES - 1)
                    for hh in range(hp):
                        add_ref[slot, hh, :, cs] = jnp.where(
                            d >= 0, jnp.take_along_axis(tabs[hh], idx, axis=1), NEG_INF)
            return carry

        lax.fori_loop(0, n_vis - n_int, fill, 0)

    if diff:
        pl.when(first)(_build_add_tiles)

    def q_rows(ref, hh):
        q = ref[0, hh]
        if not diff:
            return q
        lane = lax.broadcasted_iota(jnp.int32, (tq, LANES), 1)
        zero = jnp.zeros_like(q)
        return jnp.concatenate([jnp.where(lane < DIFF_HEAD_DIM, q, zero),
                                jnp.where(lane >= DIFF_HEAD_DIM, q, zero)], axis=0)

    m_ref[...] = jnp.full(m_ref.shape, NEG_INF, jnp.float32)
    acc_ref[...] = jnp.zeros(acc_ref.shape, jnp.float32)

    def k_off(t):
        return pl.multiple_of(tbl_ref[i * nk + t] * tk, tk)

    mxu_n = 2 * LANES
    n_sub = tk // mxu_n

    def softmax(t, s, hh, interior):
        m_old = m_ref[hh]
        if interior:
            m_cur = jnp.max(s, axis=1, keepdims=True)
            if diff:
                m_new = jnp.maximum(m_old, m_cur + far_bias[hh])
                shift = m_new - far_bias[hh]
            else:
                m_new = jnp.maximum(m_old, m_cur)
                shift = m_new
        else:
            if diff:
                add = add_ref[t - n_int, hh]
                s = s + jnp.concatenate([add, add], axis=0)
            else:
                qpos = qpos_ref[...]
                kpos = kpos_ref[tbl_ref[i * nk + t]]
                s = jnp.concatenate(
                    [jnp.where(qpos >= kpos[:, c * LANES:(c + 1) * LANES],
                               s[:, c * LANES:(c + 1) * LANES], NEG_INF) for c in range(nc)], axis=1)
            m_new = jnp.maximum(m_old, jnp.max(s, axis=1, keepdims=True))
            shift = m_new
        m_ref[hh] = m_new
        p = jnp.concatenate([jnp.exp2(s[:, c * LANES:(c + 1) * LANES] - shift)
                             for c in range(nc)], axis=1).astype(bf)
        return p, jnp.exp2(m_old - m_new)

    def visit(t_sm, s_cur, interior, t_sc, s_nxt):
        nxt = isinstance(t_sc, str)
        off_sm = None if t_sm is None else k_off(t_sm)
        if t_sc is None or (nxt and diff):
            off_sc = None
        elif nxt:
            off_sc = pl.multiple_of(tbl_ref[i_next * nk] * tk, tk)
        else:
            off_sc = k_off(t_sc)
        sm, pvs, qrows = {}, {}, {}

        def score_dot(hh, c):
            cs = slice(c * mxu_n, (c + 1) * mxu_n)
            if hh not in qrows:
                qrows[hh] = q_rows(qn_ref if nxt else q_ref, hh)
            q = qrows[hh]
            if nxt and diff:
                k = kn_ref[0, hh, cs, :]
            else:
                k = k_ref[0, hh, pl.ds(off_sc + c * mxu_n, mxu_n), :]
            s_nxt[hh, :, cs] = lax.dot_general(q, k, NT_DIMS, preferred_element_type=jnp.float32)

        def pv_dot(hh, c):
            cs = slice(c * mxu_n, (c + 1) * mxu_n)
            v = v_ref[0, hh, pl.ds(off_sm + c * mxu_n, mxu_n), :]
            d = jnp.dot(sm[hh][0][:, cs], jnp.concatenate([v, jnp.ones_like(v)], axis=1),
                        preferred_element_type=jnp.float32)
            pvs[hh] = d if hh not in pvs else pvs[hh] + d

        tiles = [(hh, c) for hh in range(hp) for c in range(n_sub)]
        for hh, c in tiles:
            if t_sm is not None and hh not in sm:
                sm[hh] = softmax(t_sm, s_cur[hh], hh, interior)
            if t_sc is not None:
                score_dot(hh, c)
            if t_sm is not None:
                pv_dot(hh, c)
                if c == n_sub - 1:
                    alpha = sm[hh][1]
                    acc_ref[hh] = jnp.concatenate([alpha, alpha], axis=1) * acc_ref[hh] + pvs[hh]

    def step(interior):
        def body(t, carry):
            for par in (0, 1):
                @pl.when((start + t) % 2 == par)
                def _():
                    visit(t, s_refs[par], interior, t + 1, s_refs[1 - par])
            return carry
        return body

    @pl.when(first)
    def _first_scores():
        visit(None, None, False, 0, s_refs[0])

    lax.fori_loop(0, n_int, step(True), 0)
    lax.fori_loop(n_int, n_vis - 1, step(False), 0)
    for par in (0, 1):
        @pl.when((start + n_vis - 1) % 2 == par)
        def _():
            visit(n_vis - 1, s_refs[par], False, "next", s_refs[1 - par])

    if diff:
        lam = (jnp.exp(jnp.sum(lq1_ref[...] * lk1_ref[...], axis=1, keepdims=True))
               - jnp.exp(jnp.sum(lq2_ref[...] * lk2_ref[...], axis=1, keepdims=True))
               + LAMBDA_INIT)
    for hh in range(hp):
        acc = acc_ref[hh]
        o = acc[:, :LANES] / acc[:, LANES:]
        if diff:
            oo = o[:tq] - lam * o[tq:]
            o = _rms(oo, subln_ref[...]) * (1.0 - LAMBDA_INIT)
        o_ref[0, :, hh * LANES:(hh + 1) * LANES] = o.astype(o_ref.dtype)


def _lane_rep(positions):
    return jnp.broadcast_to(positions[:, None], (positions.shape[0], LANES))


def _attention(q, k, v, positions, tq, tk, hp, diff_params=None):
    B, H, S, width = q.shape
    diff = diff_params is not None
    nq, nk = S // tq, S // tk
    rows = 2 * tq if diff else tq
    tbl, n_int, n_vis = _schedule(positions, tq, tk, LANES if diff else 0)
    imap = ((lambda f: (lambda g, i, b, *_: f(g, i, b))) if diff
            else (lambda f: (lambda g, b, i, *_: f(g, i, b))))
    const = imap(lambda g, i, b: (0, 0))
    nb = lambda b: jnp.minimum(b + 1, B - 1)
    ni = lambda i: jnp.minimum(i + 1, nq - 1)
    in_specs = [
        pl.BlockSpec((1, hp, tq, width), imap(lambda g, i, b: (b, g, i, 0))),
        pl.BlockSpec((1, hp, S, width), imap(lambda g, i, b: (b, g, 0, 0))),
        pl.BlockSpec((1, hp, S, LANES), imap(lambda g, i, b: (b, g, 0, 0))),
    ]
    operands = [q, k, v, q]
    scratch = [pltpu.VMEM((hp, rows, LANES), jnp.float32),
               pltpu.VMEM((hp, rows, 2 * LANES), jnp.float32),
               pltpu.VMEM((hp, rows, tk), jnp.float32),
               pltpu.VMEM((hp, rows, tk), jnp.float32)]
    if diff:
        in_specs += [
            pl.BlockSpec((1, hp, tq, width), imap(lambda g, i, b: (nb(b), g, i, 0))),
            pl.BlockSpec((1, hp, tk, width), lambda g, i, b, tbl, *_: (nb(b), g, tbl[i * nk], 0)),
        ]
        operands.append(k)
        scratch.append(pltpu.VMEM((nk, hp, tq, tk), jnp.float32))
        qt = positions.reshape(nq, tq)
        kb = positions.reshape(S // LANES, LANES)
        prefetch = [tbl, n_int, n_vis, qt.min(axis=1), qt.max(axis=1), kb.min(axis=1), kb.max(axis=1)]
    else:
        in_specs.append(pl.BlockSpec((1, hp, tq, width), imap(lambda g, i, b: (b, g, ni(i), 0))))
        start = ((jnp.cumsum(n_vis) - n_vis) % 2).astype(jnp.int32)
        prefetch = [tbl, n_int, n_vis, start]
    in_specs += [
        pl.BlockSpec((tq, LANES), imap(lambda g, i, b: (i, 0))),
        pl.BlockSpec((nk, 1, tk), imap(lambda g, i, b: (0, 0, 0))),
    ]
    operands += [_lane_rep(positions), positions.reshape(nk, 1, tk)]
    if diff:
        rel_bias, lq1, lk1, lq2, lk2, subln = diff_params
        row = lambda a: a.reshape(1, -1).astype(jnp.float32)
        prefetch.append(rel_bias.astype(jnp.float32))
        operands += [row(lq1), row(lk1), row(lq2), row(lk2), row(subln)]
        in_specs += [pl.BlockSpec((1, DIFF_HEAD_DIM), const)] * 4 + [pl.BlockSpec((1, DIFF_V_DIM), const)]
    grid_spec = pltpu.PrefetchScalarGridSpec(
        num_scalar_prefetch=len(prefetch),
        grid=(H // hp, nq, B) if diff else (H // hp, B, nq),
        in_specs=in_specs,
        out_specs=pl.BlockSpec((1, tq, hp * LANES), imap(lambda g, i, b: (b, i, g))),
        scratch_shapes=scratch,
    )
    return pl.pallas_call(
        functools.partial(_attn_kernel, tq=tq, tk=tk, nk=nk, hp=hp, diff=diff),
        grid_spec=grid_spec,
        out_shape=jax.ShapeDtypeStruct((B, S, H * LANES), jnp.bfloat16),
        compiler_params=pltpu.CompilerParams(
            dimension_semantics=("arbitrary",) * 3, vmem_limit_bytes=VMEM_LIMIT),
        name="diff_attn" if diff else "mla_attn",
    )(*prefetch, *operands)


def _mlp_kernel(x_ref, ma_ref, mb_ref, wo_ref, g1_ref, w1_ref, w2_ref, g2_ref, o_ref, *,
                ff_chunk, row_chunk):
    bf = jnp.bfloat16
    n_a = ma_ref.shape[1]
    n_ff = D_FF // ff_chunk

    def head(rs):
        x1 = (x_ref[rs, :]
              + jnp.dot(ma_ref[rs, :], wo_ref[0:n_a, :], preferred_element_type=jnp.float32)
              + jnp.dot(mb_ref[rs, :], wo_ref[n_a:, :], preferred_element_type=jnp.float32))
        return x1, _rms(x1, g1_ref[...]).astype(bf)

    def ff(h, y, c):
        sl = slice(c * ff_chunk, (c + 1) * ff_chunk)
        a = jnp.dot(h, w1_ref[:, sl], preferred_element_type=jnp.float32)
        a = jnp.square(jnp.maximum(a, 0.0)).astype(bf)
        d = jnp.dot(a, w2_ref[sl, :], preferred_element_type=jnp.float32)
        return d if y is None else y + d

    chunks = [slice(r * row_chunk, (r + 1) * row_chunk) for r in range(x_ref.shape[0] // row_chunk)]
    x1, h = head(chunks[0])
    for r, rs in enumerate(chunks):
        y = None if r == 0 else y_first
        for c in range(0 if r == 0 else 1, n_ff - 1):
            y = ff(h, y, c)
        if r + 1 < len(chunks):
            x1_n, h_n = head(chunks[r + 1])
        y = ff(h, y, n_ff - 1)
        if r + 1 < len(chunks):
            y_first = ff(h_n, None, 0)
        o_ref[rs, :] = _rms(x1 + y, g2_ref[...])
        if r + 1 < len(chunks):
            x1, h = x1_n, h_n


def _mlp(x2d, mix_a, mix_b, w_out, g1, w1, w2, g2, tile, ff_chunk, row_chunk):
    N, D = x2d.shape
    const = lambda i: (0, 0)
    single = pl.Buffered(1)
    return pl.pallas_call(
        functools.partial(_mlp_kernel, ff_chunk=ff_chunk, row_chunk=row_chunk),
        grid=(N // tile,),
        in_specs=[
            pl.BlockSpec((tile, D), lambda i: (i, 0)),
            pl.BlockSpec((tile, mix_a.shape[1]), lambda i: (i, 0)),
            pl.BlockSpec((tile, mix_b.shape[1]), lambda i: (i, 0)),
            pl.BlockSpec(w_out.shape, const, pipeline_mode=single),
            pl.BlockSpec(g1.shape, const),
            pl.BlockSpec(w1.shape, const, pipeline_mode=single),
            pl.BlockSpec(w2.shape, const, pipeline_mode=single),
            pl.BlockSpec(g2.shape, const),
        ],
        out_specs=pl.BlockSpec((tile, D), lambda i: (i, 0)),
        out_shape=jax.ShapeDtypeStruct((N, D), jnp.float32),
        compiler_params=pltpu.CompilerParams(
            dimension_semantics=("arbitrary",), vmem_limit_bytes=VMEM_LIMIT),
        name="mlp",
    )(x2d, mix_a, mix_b, w_out, g1, w1, w2, g2)


def _swap_halves(w):
    half = w.shape[-1] // 2
    return jnp.concatenate([w[..., half:], w[..., :half]], axis=-1)


def kernel(x, positions, rel_bias, norm_attn, w_in, diff_lq1, diff_lk1, diff_lq2, diff_lk2,
           diff_subln, mla_q_norm, mla_w_uq, mla_kv_norm, mla_w_ukv, w_out, norm_mlp,
           w_mlp_in, w_mlp_out, norm_final):
    B, S, D = x.shape
    bf = jnp.bfloat16
    depth = w_in.shape[0]
    assert depth == 1
    l = 0
    row = lambda a: a.reshape(1, -1).astype(jnp.float32)

    w_in_l = w_in[l].astype(bf)
    n_diff = 2 * DIFF_QK_COLS + DIFF_V_COLS
    k_pe_cols = w_in_l[:, -MLA_ROPE_DIM:]
    w_in_x = jnp.concatenate([w_in_l[:, n_diff:], _swap_halves(k_pe_cols), w_in_l[:, :n_diff]],
                             axis=1)
    w_uq = mla_w_uq[l].astype(bf).reshape(MLA_Q_RANK, MLA_HEADS, MLA_NOPE_DIM + MLA_ROPE_DIM)
    q_pe_cols = w_uq[..., MLA_NOPE_DIM:]
    w_uq_x = jnp.concatenate([w_uq, _swap_halves(q_pe_cols)], axis=-1)
    w_uq_x = w_uq_x.reshape(MLA_Q_RANK, MLA_HEADS * MLA_QK_PAD)
    w_ukv = mla_w_ukv[l].astype(bf)

    tab = _rope_table(positions, ROPE_TILE)
    dq, dk, dv, mq, mk, mv = _proj(x, row(norm_attn[l]), w_in_x, row(mla_q_norm[l]), w_uq_x,
                                   row(mla_kv_norm[l]), w_ukv, tab, PROJ_TILE)
    mix_a = _attention(dq, dk, dv, positions, DIFF_TQ, DIFF_TK, DIFF_HEADS_PER_STEP,
                       (rel_bias, diff_lq1[l], diff_lk1[l], diff_lq2[l], diff_lk2[l], diff_subln[l]))
    mix_b = _attention(mq, mk, mv, positions, MLA_TQ, MLA_TK, MLA_HEADS_PER_STEP)
    out = _mlp(x.reshape(B * S, D), mix_a.reshape(B * S, -1), mix_b.reshape(B * S, -1),
               w_out[l].astype(bf), row(norm_mlp[l]), w_mlp_in[l].astype(bf),
               w_mlp_out[l].astype(bf), row(norm_final), MLP_TILE, MLP_FF_CHUNK, MLP_ROW_CHUNK)
    return out.reshape(B, S, D)
```

```python
import functools
import math

import jax
import jax.numpy as jnp
from jax import lax
from jax.experimental import pallas as pl
from jax.experimental.pallas import tpu as pltpu

D_MODEL = 1024
DIFF_HEADS = 4
DIFF_HEAD_DIM = 64
DIFF_V_DIM = 128
DIFF_QK_COLS = DIFF_HEADS * 2 * DIFF_HEAD_DIM
DIFF_V_COLS = DIFF_HEADS * DIFF_V_DIM
MLA_HEADS = 4
MLA_Q_RANK = 384
MLA_KV_RANK = 256
MLA_NOPE_DIM = 128
MLA_ROPE_DIM = 64
MLA_V_DIM = 128
MLA_QK_PAD = 256
ROPE_BASE = 10000.0
D_FF = 4 * D_MODEL
REL_BUCKETS = 32
REL_MAX_DIST = 128
NORM_EPS = 1e-6
NEG_INF = -1e30
LOG2E = math.log2(math.e)
LAMBDA_INIT = 0.8 - 0.6 * math.exp(-0.3 * 0)

LANES = 128
VMEM_LIMIT = 56 * 1024 * 1024

ROPE_TILE = 512
PROJ_TILE = 512
DIFF_TQ, DIFF_TK, DIFF_HEADS_PER_STEP = 256, 512, 4
MLA_TQ, MLA_TK, MLA_HEADS_PER_STEP = 512, 512, 4
MLP_TILE, MLP_ROW_CHUNK, MLP_FF_CHUNK = 1024, 512, 512

NT_DIMS = (((1,), (1,)), ((), ()))


def _rms(x, gain):
    return x * lax.rsqrt(jnp.mean(x * x, axis=-1, keepdims=True) + NORM_EPS) * gain


def _rope_table_kernel(pos_ref, tab_ref):
    half = MLA_ROPE_DIM // 2
    lane = lax.broadcasted_iota(jnp.int32, (1, LANES), 1)
    fidx = (lane % half).astype(jnp.float32) * 2.0
    inv_freq = jnp.exp(-(fidx / MLA_ROPE_DIM) * math.log(ROPE_BASE))
    ang = pos_ref[...].astype(jnp.float32) * inv_freq
    c, s = jnp.cos(ang), jnp.sin(ang)
    tab_ref[...] = jnp.where(lane < 2 * half, c, jnp.where(lane < 3 * half, -s, s))


def _rope_table(positions, tile):
    seq = positions.shape[0]
    return pl.pallas_call(
        _rope_table_kernel,
        grid=(seq // tile,),
        in_specs=[pl.BlockSpec((tile, 1), lambda i: (i, 0))],
        out_specs=pl.BlockSpec((tile, LANES), lambda i: (i, 0)),
        out_shape=jax.ShapeDtypeStruct((seq, LANES), jnp.float32),
        name="rope_table",
    )(positions.reshape(seq, 1))


def _rope_slab(slab, tab):
    prod = slab * tab
    return prod + pltpu.roll(prod, MLA_ROPE_DIM, axis=1)


def _proj_kernel(x_ref, g_ref, w_in_ref, qn_ref, w_uq_ref, kvn_ref, w_ukv_ref, tab_ref,
                 dq_ref, dk_ref, dv_ref, mq_ref, mk_ref, mv_ref):
    bf = jnp.bfloat16
    x = x_ref[0]
    h = (x * g_ref[...]).astype(bf)
    r = lax.rsqrt(jnp.mean(x * x, axis=-1, keepdims=True) + NORM_EPS)
    o_ckv = MLA_Q_RANK
    o_kpe = o_ckv + MLA_KV_RANK
    o_dq = o_kpe + LANES
    o_dk = o_dq + DIFF_QK_COLS
    o_dv = o_dk + DIFF_QK_COLS
    p = jnp.dot(h, w_in_ref[:, :o_dq], preferred_element_type=jnp.float32) * r
    tab = tab_ref[...]
    lane = lax.broadcasted_iota(jnp.int32, tab.shape, 1)

    dscale = (DIFF_HEAD_DIM ** -0.5) * LOG2E
    pq = jnp.dot(h, w_in_ref[:, o_dq:o_dk], preferred_element_type=jnp.float32)
    rq = r * dscale
    for hh in range(DIFF_HEADS):
        dq_ref[0, hh] = (pq[:, hh * LANES:(hh + 1) * LANES] * rq).astype(bf)

    cq = _rms(p[:, :o_ckv], qn_ref[...]).astype(bf)
    q = jnp.dot(cq, w_uq_ref[...], preferred_element_type=jnp.float32)
    ckv = _rms(p[:, o_ckv:o_kpe], kvn_ref[...]).astype(bf)
    kv = jnp.dot(ckv, w_ukv_ref[...], preferred_element_type=jnp.float32)
    k_rope = _rope_slab(p[:, o_kpe:o_dq], tab).astype(bf)

    pkv = jnp.dot(h, w_in_ref[:, o_dk:], preferred_element_type=jnp.float32)
    for hh in range(DIFF_HEADS):
        dk_ref[0, hh] = (pkv[:, hh * LANES:(hh + 1) * LANES] * r).astype(bf)
        dv_ref[0, hh] = (pkv[:, DIFF_QK_COLS + hh * LANES:DIFF_QK_COLS + (hh + 1) * LANES]
                         * r).astype(bf)

    mscale = ((MLA_NOPE_DIM + MLA_ROPE_DIM) ** -0.5) * LOG2E
    for hh in range(MLA_HEADS):
        base = hh * MLA_QK_PAD
        q_rope = _rope_slab(q[:, base + LANES:base + 2 * LANES], tab)
        q_rope = jnp.where(lane < MLA_ROPE_DIM, q_rope, 0.0)
        mq_ref[0, hh, :, 0:LANES] = (q[:, base:base + LANES] * mscale).astype(bf)
        mq_ref[0, hh, :, LANES:2 * LANES] = (q_rope * mscale).astype(bf)
        mk_ref[0, hh, :, 0:LANES] = kv[:, base:base + LANES].astype(bf)
        mk_ref[0, hh, :, LANES:2 * LANES] = k_rope
        mv_ref[0, hh] = kv[:, base + LANES:base + 2 * LANES].astype(bf)


def _proj(x, g, w_in, qn, w_uq, kvn, w_ukv, tab, tile):
    B, S, D = x.shape
    bf = jnp.bfloat16
    const = lambda b, i: (0, 0)
    head_out = lambda w: pl.BlockSpec((1, DIFF_HEADS, tile, w), lambda b, i: (b, 0, i, 0))
    shp = lambda w: jax.ShapeDtypeStruct((B, DIFF_HEADS, S, w), bf)
    return pl.pallas_call(
        _proj_kernel,
        grid=(B, S // tile),
        in_specs=[
            pl.BlockSpec((1, tile, D), lambda b, i: (b, i, 0)),
            pl.BlockSpec(g.shape, const),
            pl.BlockSpec(w_in.shape, const),
            pl.BlockSpec(qn.shape, const),
            pl.BlockSpec(w_uq.shape, const),
            pl.BlockSpec(kvn.shape, const),
            pl.BlockSpec(w_ukv.shape, const),
            pl.BlockSpec((tile, LANES), lambda b, i: (i, 0)),
        ],
        out_specs=[head_out(LANES), head_out(LANES), head_out(LANES),
                   head_out(MLA_QK_PAD), head_out(MLA_QK_PAD), head_out(LANES)],
        out_shape=[shp(LANES), shp(LANES), shp(LANES),
                   shp(MLA_QK_PAD), shp(MLA_QK_PAD), shp(LANES)],
        compiler_params=pltpu.CompilerParams(
            dimension_semantics=("arbitrary", "arbitrary"), vmem_limit_bytes=VMEM_LIMIT),
        name="proj",
    )(x, g, w_in, qn, w_uq, kvn, w_ukv, tab)


def _schedule(positions, tq, tk, far):
    seq = positions.shape[0]
    qp = positions.reshape(seq // tq, tq)
    kp = positions.reshape(seq // tk, tk)
    qmin, qmax = qp.min(axis=1), qp.max(axis=1)
    kmin, kmax = kp.min(axis=1), kp.max(axis=1)
    skip = kmin[None, :] > qmax[:, None]
    interior = (qmin[:, None] - kmax[None, :]) >= far
    cls = jnp.where(interior, 0, jnp.where(skip, 2, 1)).astype(jnp.int32)
    order = jnp.argsort(cls, axis=1, stable=True).astype(jnp.int32)
    n_int = jnp.sum(cls == 0, axis=1).astype(jnp.int32)
    n_vis = jnp.sum(cls < 2, axis=1).astype(jnp.int32)
    return order.reshape(-1), n_int, n_vis


def _t5_bias_row(relb_ref, head):
    n = lax.broadcasted_iota(jnp.int32, (1, LANES), 1)
    max_exact = REL_BUCKETS // 2
    nf = jnp.maximum(n, 1).astype(jnp.float32)
    large = max_exact + jnp.floor(jnp.log(nf / max_exact) / math.log(REL_MAX_DIST / max_exact)
                                  * (REL_BUCKETS - max_exact)).astype(jnp.int32)
    large = jnp.minimum(large, REL_BUCKETS - 1)
    bucket = jnp.where(n < max_exact, n, large)
    row = jnp.zeros((1, LANES), jnp.float32)
    for b in range(REL_BUCKETS):
        row = jnp.where(bucket == b, relb_ref[b, head] * LOG2E, row)
    return row


def _attn_kernel(*refs, tq, tk, nk, hp, diff):
    if diff:
        (tbl_ref, nint_ref, nvis_ref, qmin_ref, qmax_ref, kbmin_ref, kbmax_ref, live_ref, relb_ref,
         q_ref, k_ref, v_ref, qn_ref, kn_ref,
         qpos_ref, kpos_ref, lq1_ref, lk1_ref, lq2_ref, lk2_ref, subln_ref,
         o_ref, m_ref, acc_ref, s0_ref, s1_ref, add_ref) = refs
        grp, i, batch = pl.program_id(0), pl.program_id(1), pl.program_id(2)
        n_vis = nvis_ref[i]
        first = batch == 0
        start = (batch * n_vis) % 2
    else:
        (tbl_ref, nint_ref, nvis_ref, start_ref, q_ref, k_ref, v_ref, qn_ref, qpos_ref, kpos_ref,
         o_ref, m_ref, acc_ref, s0_ref, s1_ref) = refs
        grp, batch, i = pl.program_id(0), pl.program_id(1), pl.program_id(2)
        n_vis = nvis_ref[i]
        first = i == 0
        start = start_ref[i]
        i_next = jnp.minimum(i + 1, pl.num_programs(2) - 1)
    s_refs = (s0_ref, s1_ref)
    n_int = jnp.minimum(nint_ref[i], n_vis - 1)
    nc = tk // LANES
    bf = jnp.bfloat16

    if diff:
        far_bias = [relb_ref[REL_BUCKETS - 1, grp * hp + hh] * LOG2E for hh in range(hp)]

    def _build_add_tiles():
        qpos = qpos_ref[...]
        tabs = [jnp.broadcast_to(_t5_bias_row(relb_ref, grp * hp + hh), (tq, LANES))
                for hh in range(hp)]

        def fill(slot, carry):
            j = tbl_ref[i * nk + n_int + slot]
            kpos = kpos_ref[j]
            for c in range(nc):
                cs = slice(c * LANES, (c + 1) * LANES)
                blk = j * nc + c
                far = qmin_ref[i] - kbmax_ref[blk] >= LANES - 1
                dead = kbmin_ref[blk] > qmax_ref[i]

                @pl.when(far)
                def _():
                    for hh in range(hp):
                        add_ref[slot, hh, :, cs] = jnp.full((tq, LANES), far_bias[hh], jnp.float32)

                @pl.when(dead)
                def _():
                    for hh in range(hp):
                        add_ref[slot, hh, :, cs] = jnp.full((tq, LANES), NEG_INF, jnp.float32)

                @pl.when(jnp.logical_not(jnp.logical_or(far, dead)))
                def _():
                    d = qpos - kpos[:, cs]
                    idx = jnp.clip(d, 0, LANES - 1)
                    for hh in range(hp):
                        add_ref[slot, hh, :, cs] = jnp.where(
                            d >= 0, jnp.take_along_axis(tabs[hh], idx, axis=1), NEG_INF)
            return carry

        lax.fori_loop(0, n_vis - n_int, fill, 0)

    if diff:
        pl.when(first)(_build_add_tiles)

    def q_rows(ref, hh):
        q = ref[0, hh]
        if not diff:
            return q
        lane = lax.broadcasted_iota(jnp.int32, (tq, LANES), 1)
        zero = jnp.zeros_like(q)
        return jnp.concatenate([jnp.where(lane < DIFF_HEAD_DIM, q, zero),
                                jnp.where(lane >= DIFF_HEAD_DIM, q, zero)], axis=0)

    m_ref[...] = jnp.full(m_ref.shape, NEG_INF, jnp.float32)
    acc_ref[...] = jnp.zeros(acc_ref.shape, jnp.float32)

    def k_off(t):
        return pl.multiple_of(tbl_ref[i * nk + t] * tk, tk)

    mxu_n = 2 * LANES
    n_sub = tk // mxu_n

    def softmax(t, s_ref, hh, interior, n_live):
        cols = n_live * mxu_n
        nc = cols // LANES
        s = s_ref[hh, :, :cols]
        m_old = m_ref[hh]
        if interior:
            m_cur = jnp.max(s, axis=1, keepdims=True)
            if diff:
                m_new = jnp.maximum(m_old, m_cur + far_bias[hh])
                shift = m_new - far_bias[hh]
            else:
                m_new = jnp.maximum(m_old, m_cur)
                shift = m_new
        else:
            if diff:
                add = add_ref[t - n_int, hh, :, :cols]
                s = s + jnp.concatenate([add, add], axis=0)
            else:
                qpos = qpos_ref[...]
                kpos = kpos_ref[tbl_ref[i * nk + t]]
                s = jnp.concatenate(
                    [jnp.where(qpos >= kpos[:, c * LANES:(c + 1) * LANES],
                               s[:, c * LANES:(c + 1) * LANES], NEG_INF) for c in range(nc)], axis=1)
            m_new = jnp.maximum(m_old, jnp.max(s, axis=1, keepdims=True))
            shift = m_new
        m_ref[hh] = m_new
        p = jnp.concatenate([jnp.exp2(s[:, c * LANES:(c + 1) * LANES] - shift)
                             for c in range(nc)], axis=1).astype(bf)
        return p, jnp.exp2(m_old - m_new)

    def visit(t_sm, s_cur, interior, t_sc, s_nxt, n_live=n_sub):
        nxt = isinstance(t_sc, str)
        off_sm = None if t_sm is None else k_off(t_sm)
        if t_sc is None or (nxt and diff):
            off_sc = None
        elif nxt:
            off_sc = pl.multiple_of(tbl_ref[i_next * nk] * tk, tk)
        else:
            off_sc = k_off(t_sc)
        sm, pvs, qrows = {}, {}, {}

        def score_dot(hh, c):
            cs = slice(c * mxu_n, (c + 1) * mxu_n)
            if hh not in qrows:
                qrows[hh] = q_rows(qn_ref if nxt else q_ref, hh)
            q = qrows[hh]
            if nxt and diff:
                k = kn_ref[0, hh, cs, :]
            else:
                k = k_ref[0, hh, pl.ds(off_sc + c * mxu_n, mxu_n), :]
            s_nxt[hh, :, cs] = lax.dot_general(q, k, NT_DIMS, preferred_element_type=jnp.float32)

        def pv_dot(hh, c):
            cs = slice(c * mxu_n, (c + 1) * mxu_n)
            v = v_ref[0, hh, pl.ds(off_sm + c * mxu_n, mxu_n), :]
            d = jnp.dot(sm[hh][0][:, cs], jnp.concatenate([v, jnp.ones_like(v)], axis=1),
                        preferred_element_type=jnp.float32)
            pvs[hh] = d if hh not in pvs else pvs[hh] + d

        tiles = [(hh, c) for hh in range(hp) for c in range(n_sub)]
        for hh, c in tiles:
            if t_sm is not None and hh not in sm:
                sm[hh] = softmax(t_sm, s_cur, hh, interior, n_live)
            if t_sc is not None:
                score_dot(hh, c)
            if t_sm is not None and c < n_live:
                pv_dot(hh, c)
                if c == n_live - 1:
                    alpha = sm[hh][1]
                    acc_ref[hh] = jnp.concatenate([alpha, alpha], axis=1) * acc_ref[hh] + pvs[hh]

    def step(interior):
        def body(t, carry):
            for par in (0, 1):
                @pl.when((start + t) % 2 == par)
                def _():
                    visit(t, s_refs[par], interior, t + 1, s_refs[1 - par])
            return carry
        return body

    @pl.when(first)
    def _first_scores():
        visit(None, None, False, 0, s_refs[0])

    lax.fori_loop(0, n_int, step(True), 0)
    lax.fori_loop(n_int, n_vis - 1, step(False), 0)
    live_options = range(1, n_sub + 1) if diff else (n_sub,)
    for par in (0, 1):
        for n_live in live_options:
            cond = (start + n_vis - 1) % 2 == par
            if diff:
                cond = jnp.logical_and(cond, live_ref[i] == n_live)

            @pl.when(cond)
            def _():
                visit(n_vis - 1, s_refs[par], False, "next", s_refs[1 - par], n_live)

    if diff:
        lam = (jnp.exp(jnp.sum(lq1_ref[...] * lk1_ref[...], axis=1, keepdims=True))
               - jnp.exp(jnp.sum(lq2_ref[...] * lk2_ref[...], axis=1, keepdims=True))
               + LAMBDA_INIT)
    for hh in range(hp):
        acc = acc_ref[hh]
        o = acc[:, :LANES] / acc[:, LANES:]
        if diff:
            oo = o[:tq] - lam * o[tq:]
            o = _rms(oo, subln_ref[...]) * (1.0 - LAMBDA_INIT)
        o_ref[0, :, hh * LANES:(hh + 1) * LANES] = o.astype(o_ref.dtype)


def _lane_rep(positions):
    return jnp.broadcast_to(positions[:, None], (positions.shape[0], LANES))


def _attention(q, k, v, positions, tq, tk, hp, diff_params=None):
    B, H, S, width = q.shape
    diff = diff_params is not None
    nq, nk = S // tq, S // tk
    rows = 2 * tq if diff else tq
    tbl, n_int, n_vis = _schedule(positions, tq, tk, LANES if diff else 0)
    imap = ((lambda f: (lambda g, i, b, *_: f(g, i, b))) if diff
            else (lambda f: (lambda g, b, i, *_: f(g, i, b))))
    const = imap(lambda g, i, b: (0, 0))
    nb = lambda b: jnp.minimum(b + 1, B - 1)
    ni = lambda i: jnp.minimum(i + 1, nq - 1)
    in_specs = [
        pl.BlockSpec((1, hp, tq, width), imap(lambda g, i, b: (b, g, i, 0))),
        pl.BlockSpec((1, hp, S, width), imap(lambda g, i, b: (b, g, 0, 0))),
        pl.BlockSpec((1, hp, S, LANES), imap(lambda g, i, b: (b, g, 0, 0))),
    ]
    operands = [q, k, v, q]
    scratch = [pltpu.VMEM((hp, rows, LANES), jnp.float32),
               pltpu.VMEM((hp, rows, 2 * LANES), jnp.float32),
               pltpu.VMEM((hp, rows, tk), jnp.float32),
               pltpu.VMEM((hp, rows, tk), jnp.float32)]
    if diff:
        in_specs += [
            pl.BlockSpec((1, hp, tq, width), imap(lambda g, i, b: (nb(b), g, i, 0))),
            pl.BlockSpec((1, hp, tk, width), lambda g, i, b, tbl, *_: (nb(b), g, tbl[i * nk], 0)),
        ]
        operands.append(k)
        scratch.append(pltpu.VMEM((nk, hp, tq, tk), jnp.float32))
        qt = positions.reshape(nq, tq)
        qmin, qmax = qt.min(axis=1), qt.max(axis=1)
        kb = positions.reshape(S // LANES, LANES)
        n_sub = tk // (2 * LANES)
        j_last = tbl.reshape(nq, nk)[jnp.arange(nq), n_vis - 1]
        sub_min = positions.reshape(nk, n_sub, 2 * LANES).min(axis=2)[j_last]
        dead = (sub_min > qmax[:, None]).astype(jnp.int32)
        trailing_dead = jnp.cumprod(dead[:, ::-1], axis=1).sum(axis=1)
        live = jnp.maximum(n_sub - trailing_dead, 1).astype(jnp.int32)
        prefetch = [tbl, n_int, n_vis, qmin, qmax, kb.min(axis=1), kb.max(axis=1), live]
    else:
        in_specs.append(pl.BlockSpec((1, hp, tq, width), imap(lambda g, i, b: (b, g, ni(i), 0))))
        start = ((jnp.cumsum(n_vis) - n_vis) % 2).astype(jnp.int32)
        prefetch = [tbl, n_int, n_vis, start]
    in_specs += [
        pl.BlockSpec((tq, LANES), imap(lambda g, i, b: (i, 0))),
        pl.BlockSpec((nk, 1, tk), imap(lambda g, i, b: (0, 0, 0))),
    ]
    operands += [_lane_rep(positions), positions.reshape(nk, 1, tk)]
    if diff:
        rel_bias, lq1, lk1, lq2, lk2, subln = diff_params
        row = lambda a: a.reshape(1, -1).astype(jnp.float32)
        prefetch.append(rel_bias.astype(jnp.float32))
        operands += [row(lq1), row(lk1), row(lq2), row(lk2), row(subln)]
        in_specs += [pl.BlockSpec((1, DIFF_HEAD_DIM), const)] * 4 + [pl.BlockSpec((1, DIFF_V_DIM), const)]
    grid_spec = pltpu.PrefetchScalarGridSpec(
        num_scalar_prefetch=len(prefetch),
        grid=(H // hp, nq, B) if diff else (H // hp, B, nq),
        in_specs=in_specs,
        out_specs=pl.BlockSpec((1, tq, hp * LANES), imap(lambda g, i, b: (b, i, g))),
        scratch_shapes=scratch,
    )
    return pl.pallas_call(
        functools.partial(_attn_kernel, tq=tq, tk=tk, nk=nk, hp=hp, diff=diff),
        grid_spec=grid_spec,
        out_shape=jax.ShapeDtypeStruct((B, S, H * LANES), jnp.bfloat16),
        compiler_params=pltpu.CompilerParams(
            dimension_semantics=("arbitrary",) * 3, vmem_limit_bytes=VMEM_LIMIT),
        name="diff_attn" if diff else "mla_attn",
    )(*prefetch, *operands)


def _mlp_kernel(x_ref, ma_ref, mb_ref, wo_ref, g1_ref, w1_ref, w2_ref, g2_ref, o_ref, *,
                ff_chunk, row_chunk):
    bf = jnp.bfloat16
    n_a = ma_ref.shape[1]
    n_ff = D_FF // ff_chunk

    def head(rs):
        x1 = (x_ref[rs, :]
              + jnp.dot(ma_ref[rs, :], wo_ref[0:n_a, :], preferred_element_type=jnp.float32)
              + jnp.dot(mb_ref[rs, :], wo_ref[n_a:, :], preferred_element_type=jnp.float32))
        return x1, _rms(x1, g1_ref[...]).astype(bf)

    def ff(h, y, c):
        sl = slice(c * ff_chunk, (c + 1) * ff_chunk)
        a = jnp.dot(h, w1_ref[:, sl], preferred_element_type=jnp.float32)
        a = jnp.square(jnp.maximum(a, 0.0)).astype(bf)
        d = jnp.dot(a, w2_ref[sl, :], preferred_element_type=jnp.float32)
        return d if y is None else y + d

    chunks = [slice(r * row_chunk, (r + 1) * row_chunk) for r in range(x_ref.shape[0] // row_chunk)]
    x1, h = head(chunks[0])
    for r, rs in enumerate(chunks):
        y = None if r == 0 else y_first
        for c in range(0 if r == 0 else 1, n_ff - 1):
            y = ff(h, y, c)
        if r + 1 < len(chunks):
            x1_n, h_n = head(chunks[r + 1])
        y = ff(h, y, n_ff - 1)
        if r + 1 < len(chunks):
            y_first = ff(h_n, None, 0)
        o_ref[rs, :] = _rms(x1 + y, g2_ref[...])
        if r + 1 < len(chunks):
            x1, h = x1_n, h_n


def _mlp(x2d, mix_a, mix_b, w_out, g1, w1, w2, g2, tile, ff_chunk, row_chunk):
    N, D = x2d.shape
    const = lambda i: (0, 0)
    single = pl.Buffered(1)
    return pl.pallas_call(
        functools.partial(_mlp_kernel, ff_chunk=ff_chunk, row_chunk=row_chunk),
        grid=(N // tile,),
        in_specs=[
            pl.BlockSpec((tile, D), lambda i: (i, 0)),
            pl.BlockSpec((tile, mix_a.shape[1]), lambda i: (i, 0)),
            pl.BlockSpec((tile, mix_b.shape[1]), lambda i: (i, 0)),
            pl.BlockSpec(w_out.shape, const, pipeline_mode=single),
            pl.BlockSpec(g1.shape, const),
            pl.BlockSpec(w1.shape, const, pipeline_mode=single),
            pl.BlockSpec(w2.shape, const, pipeline_mode=single),
            pl.BlockSpec(g2.shape, const),
        ],
        out_specs=pl.BlockSpec((tile, D), lambda i: (i, 0)),
        out_shape=jax.ShapeDtypeStruct((N, D), jnp.float32),
        compiler_params=pltpu.CompilerParams(
            dimension_semantics=("arbitrary",), vmem_limit_bytes=VMEM_LIMIT),
        name="mlp",
    )(x2d, mix_a, mix_b, w_out, g1, w1, w2, g2)


def _swap_halves(w):
    half = w.shape[-1] // 2
    return jnp.concatenate([w[..., half:], w[..., :half]], axis=-1)


def kernel(x, positions, rel_bias, norm_attn, w_in, diff_lq1, diff_lk1, diff_lq2, diff_lk2,
           diff_subln, mla_q_norm, mla_w_uq, mla_kv_norm, mla_w_ukv, w_out, norm_mlp,
           w_mlp_in, w_mlp_out, norm_final):
    B, S, D = x.shape
    bf = jnp.bfloat16
    depth = w_in.shape[0]
    assert depth == 1
    l = 0
    row = lambda a: a.reshape(1, -1).astype(jnp.float32)

    w_in_l = w_in[l].astype(bf)
    n_diff = 2 * DIFF_QK_COLS + DIFF_V_COLS
    k_pe_cols = w_in_l[:, -MLA_ROPE_DIM:]
    w_in_x = jnp.concatenate([w_in_l[:, n_diff:], _swap_halves(k_pe_cols), w_in_l[:, :n_diff]],
                             axis=1)
    w_uq = mla_w_uq[l].astype(bf).reshape(MLA_Q_RANK, MLA_HEADS, MLA_NOPE_DIM + MLA_ROPE_DIM)
    q_pe_cols = w_uq[..., MLA_NOPE_DIM:]
    w_uq_x = jnp.concatenate([w_uq, _swap_halves(q_pe_cols)], axis=-1)
    w_uq_x = w_uq_x.reshape(MLA_Q_RANK, MLA_HEADS * MLA_QK_PAD)
    w_ukv = mla_w_ukv[l].astype(bf)

    tab = _rope_table(positions, ROPE_TILE)
    dq, dk, dv, mq, mk, mv = _proj(x, row(norm_attn[l]), w_in_x, row(mla_q_norm[l]), w_uq_x,
                                   row(mla_kv_norm[l]), w_ukv, tab, PROJ_TILE)
    mix_a = _attention(dq, dk, dv, positions, DIFF_TQ, DIFF_TK, DIFF_HEADS_PER_STEP,
                       (rel_bias, diff_lq1[l], diff_lk1[l], diff_lq2[l], diff_lk2[l], diff_subln[l]))
    mix_b = _attention(mq, mk, mv, positions, MLA_TQ, MLA_TK, MLA_HEADS_PER_STEP)
    out = _mlp(x.reshape(B * S, D), mix_a.reshape(B * S, -1), mix_b.reshape(B * S, -1),
               w_out[l].astype(bf), row(norm_mlp[l]), w_mlp_in[l].astype(bf),
               w_mlp_out[l].astype(bf), row(norm_final), MLP_TILE, MLP_FF_CHUNK, MLP_ROW_CHUNK)
    return out.reshape(B, S, D)
```

```python
import functools
import math

import jax
import jax.numpy as jnp
from jax import lax
from jax.experimental import pallas as pl
from jax.experimental.pallas import tpu as pltpu

D_MODEL = 1024
DIFF_HEADS = 4
DIFF_HEAD_DIM = 64
DIFF_V_DIM = 128
DIFF_QK_COLS = DIFF_HEADS * 2 * DIFF_HEAD_DIM
DIFF_V_COLS = DIFF_HEADS * DIFF_V_DIM
MLA_HEADS = 4
MLA_Q_RANK = 384
MLA_KV_RANK = 256
MLA_NOPE_DIM = 128
MLA_ROPE_DIM = 64
MLA_V_DIM = 128
MLA_QK_PAD = 256
ROPE_BASE = 10000.0
D_FF = 4 * D_MODEL
REL_BUCKETS = 32
REL_MAX_DIST = 128
NORM_EPS = 1e-6
NEG_INF = -1e30
LOG2E = math.log2(math.e)
LAMBDA_INIT = 0.8 - 0.6 * math.exp(-0.3 * 0)

LANES = 128
VMEM_LIMIT = 56 * 1024 * 1024

ROPE_TILE = 512
PROJ_TILE = 512
DIFF_TQ, DIFF_TK, DIFF_HEADS_PER_STEP = 256, 512, 4
MLA_TQ, MLA_TK, MLA_HEADS_PER_STEP = 512, 512, 4
MLP_TILE, MLP_ROW_CHUNK, MLP_FF_CHUNK = 1024, 512, 512

NT_DIMS = (((1,), (1,)), ((), ()))


def _rms(x, gain):
    return x * lax.rsqrt(jnp.mean(x * x, axis=-1, keepdims=True) + NORM_EPS) * gain


def _rope_table_kernel(pos_ref, tab_ref):
    half = MLA_ROPE_DIM // 2
    lane = lax.broadcasted_iota(jnp.int32, (1, LANES), 1)
    fidx = (lane % half).astype(jnp.float32) * 2.0
    inv_freq = jnp.exp(-(fidx / MLA_ROPE_DIM) * math.log(ROPE_BASE))
    ang = pos_ref[...].astype(jnp.float32) * inv_freq
    c, s = jnp.cos(ang), jnp.sin(ang)
    tab_ref[...] = jnp.where(lane < 2 * half, c, jnp.where(lane < 3 * half, -s, s))


def _rope_table(positions, tile):
    seq = positions.shape[0]
    return pl.pallas_call(
        _rope_table_kernel,
        grid=(seq // tile,),
        in_specs=[pl.BlockSpec((tile, 1), lambda i: (i, 0))],
        out_specs=pl.BlockSpec((tile, LANES), lambda i: (i, 0)),
        out_shape=jax.ShapeDtypeStruct((seq, LANES), jnp.float32),
        name="rope_table",
    )(positions.reshape(seq, 1))


def _rope_slab(slab, tab):
    prod = slab * tab
    return prod + pltpu.roll(prod, MLA_ROPE_DIM, axis=1)


def _proj_kernel(x_ref, g_ref, w_in_ref, qn_ref, w_uq_ref, kvn_ref, w_ukv_ref, tab_ref,
                 dq_ref, dk_ref, dv_ref, mq_ref, mk_ref, mv_ref):
    bf = jnp.bfloat16
    x = x_ref[0]
    h = (x * g_ref[...]).astype(bf)
    r = lax.rsqrt(jnp.mean(x * x, axis=-1, keepdims=True) + NORM_EPS)
    o_ckv = MLA_Q_RANK
    o_kpe = o_ckv + MLA_KV_RANK
    o_dq = o_kpe + LANES
    o_dk = o_dq + DIFF_QK_COLS
    o_dv = o_dk + DIFF_QK_COLS
    p = jnp.dot(h, w_in_ref[:, :o_dq], preferred_element_type=jnp.float32) * r
    tab = tab_ref[...]
    lane = lax.broadcasted_iota(jnp.int32, tab.shape, 1)

    dscale = (DIFF_HEAD_DIM ** -0.5) * LOG2E
    pq = jnp.dot(h, w_in_ref[:, o_dq:o_dk], preferred_element_type=jnp.float32)
    rq = r * dscale
    for hh in range(DIFF_HEADS):
        dq_ref[0, hh] = (pq[:, hh * LANES:(hh + 1) * LANES] * rq).astype(bf)

    cq = _rms(p[:, :o_ckv], qn_ref[...]).astype(bf)
    q = jnp.dot(cq, w_uq_ref[...], preferred_element_type=jnp.float32)
    ckv = _rms(p[:, o_ckv:o_kpe], kvn_ref[...]).astype(bf)
    kv = jnp.dot(ckv, w_ukv_ref[...], preferred_element_type=jnp.float32)
    k_rope = _rope_slab(p[:, o_kpe:o_dq], tab).astype(bf)

    pkv = jnp.dot(h, w_in_ref[:, o_dk:], preferred_element_type=jnp.float32)
    for hh in range(DIFF_HEADS):
        dk_ref[0, hh] = (pkv[:, hh * LANES:(hh + 1) * LANES] * r).astype(bf)
        dv_ref[0, hh] = (pkv[:, DIFF_QK_COLS + hh * LANES:DIFF_QK_COLS + (hh + 1) * LANES]
                         * r).astype(bf)

    mscale = ((MLA_NOPE_DIM + MLA_ROPE_DIM) ** -0.5) * LOG2E
    for hh in range(MLA_HEADS):
        base = hh * MLA_QK_PAD
        q_rope = _rope_slab(q[:, base + LANES:base + 2 * LANES], tab)
        q_rope = jnp.where(lane < MLA_ROPE_DIM, q_rope, 0.0)
        mq_ref[0, hh, :, 0:LANES] = (q[:, base:base + LANES] * mscale).astype(bf)
        mq_ref[0, hh, :, LANES:2 * LANES] = (q_rope * mscale).astype(bf)
        mk_ref[0, hh, :, 0:LANES] = kv[:, base:base + LANES].astype(bf)
        mk_ref[0, hh, :, LANES:2 * LANES] = k_rope
        mv_ref[0, hh] = kv[:, base + LANES:base + 2 * LANES].astype(bf)


def _proj(x, g, w_in, qn, w_uq, kvn, w_ukv, tab, tile):
    B, S, D = x.shape
    bf = jnp.bfloat16
    const = lambda b, i: (0, 0)
    head_out = lambda w: pl.BlockSpec((1, DIFF_HEADS, tile, w), lambda b, i: (b, 0, i, 0))
    shp = lambda w: jax.ShapeDtypeStruct((B, DIFF_HEADS, S, w), bf)
    return pl.pallas_call(
        _proj_kernel,
        grid=(B, S // tile),
        in_specs=[
            pl.BlockSpec((1, tile, D), lambda b, i: (b, i, 0)),
            pl.BlockSpec(g.shape, const),
            pl.BlockSpec(w_in.shape, const),
            pl.BlockSpec(qn.shape, const),
            pl.BlockSpec(w_uq.shape, const),
            pl.BlockSpec(kvn.shape, const),
            pl.BlockSpec(w_ukv.shape, const),
            pl.BlockSpec((tile, LANES), lambda b, i: (i, 0)),
        ],
        out_specs=[head_out(LANES), head_out(LANES), head_out(LANES),
                   head_out(MLA_QK_PAD), head_out(MLA_QK_PAD), head_out(LANES)],
        out_shape=[shp(LANES), shp(LANES), shp(LANES),
                   shp(MLA_QK_PAD), shp(MLA_QK_PAD), shp(LANES)],
        compiler_params=pltpu.CompilerParams(
            dimension_semantics=("arbitrary", "arbitrary"), vmem_limit_bytes=VMEM_LIMIT),
        name="proj",
    )(x, g, w_in, qn, w_uq, kvn, w_ukv, tab)


def _schedule(positions, tq, tk, far):
    seq = positions.shape[0]
    qp = positions.reshape(seq // tq, tq)
    kp = positions.reshape(seq // tk, tk)
    qmin, qmax = qp.min(axis=1), qp.max(axis=1)
    kmin, kmax = kp.min(axis=1), kp.max(axis=1)
    skip = kmin[None, :] > qmax[:, None]
    interior = (qmin[:, None] - kmax[None, :]) >= far
    cls = jnp.where(interior, 0, jnp.where(skip, 2, 1)).astype(jnp.int32)
    order = jnp.argsort(cls, axis=1, stable=True).astype(jnp.int32)
    n_int = jnp.sum(cls == 0, axis=1).astype(jnp.int32)
    n_vis = jnp.sum(cls < 2, axis=1).astype(jnp.int32)
    return order.reshape(-1), n_int, n_vis


def _t5_bias_row(relb_ref, head):
    n = lax.broadcasted_iota(jnp.int32, (1, LANES), 1)
    max_exact = REL_BUCKETS // 2
    nf = jnp.maximum(n, 1).astype(jnp.float32)
    large = max_exact + jnp.floor(jnp.log(nf / max_exact) / math.log(REL_MAX_DIST / max_exact)
                                  * (REL_BUCKETS - max_exact)).astype(jnp.int32)
    large = jnp.minimum(large, REL_BUCKETS - 1)
    bucket = jnp.where(n < max_exact, n, large)
    row = jnp.zeros((1, LANES), jnp.float32)
    for b in range(REL_BUCKETS):
        row = jnp.where(bucket == b, relb_ref[b, head] * LOG2E, row)
    return row


def _attn_kernel(*refs, tq, tk, nk, hp, diff):
    if diff:
        (tbl_ref, nint_ref, nvis_ref, base_ref, qmin_ref, qmax_ref, kbmin_ref, kbmax_ref, relb_ref,
         q_ref, k_ref, v_ref, qn_ref, kn_ref,
         qpos_ref, kpos_ref, lq1_ref, lk1_ref, lq2_ref, lk2_ref, subln_ref,
         o_ref, m_ref, acc_ref, s0_ref, s1_ref, add_ref) = refs
        grp, i, batch = pl.program_id(0), pl.program_id(1), pl.program_id(2)
        n_vis = nvis_ref[i]
        first = jnp.logical_and(i == 0, batch == 0)
        start = (base_ref[i] + batch * n_vis) % 2
    else:
        (tbl_ref, nint_ref, nvis_ref, start_ref, q_ref, k_ref, v_ref, qn_ref, qpos_ref, kpos_ref,
         o_ref, m_ref, acc_ref, s0_ref, s1_ref) = refs
        grp, batch, i = pl.program_id(0), pl.program_id(1), pl.program_id(2)
        n_vis = nvis_ref[i]
        first = i == 0
        start = start_ref[i]
        i_next = jnp.minimum(i + 1, pl.num_programs(2) - 1)
    s_refs = (s0_ref, s1_ref)
    n_int = jnp.minimum(nint_ref[i], n_vis - 1)
    nc = tk // LANES
    bf = jnp.bfloat16

    if diff:
        far_bias = [relb_ref[REL_BUCKETS - 1, grp * hp + hh] * LOG2E for hh in range(hp)]

    def _build_add_tiles():
        qpos = qpos_ref[...]
        tabs = [jnp.broadcast_to(_t5_bias_row(relb_ref, grp * hp + hh), (tq, LANES))
                for hh in range(hp)]

        def fill(slot, carry):
            j = tbl_ref[i * nk + n_int + slot]
            kpos = kpos_ref[j]
            for c in range(nc):
                cs = slice(c * LANES, (c + 1) * LANES)
                blk = j * nc + c
                far = qmin_ref[i] - kbmax_ref[blk] >= LANES - 1
                dead = kbmin_ref[blk] > qmax_ref[i]

                @pl.when(far)
                def _():
                    for hh in range(hp):
                        add_ref[slot, hh, :, cs] = jnp.full((tq, LANES), far_bias[hh], jnp.float32)

                @pl.when(dead)
                def _():
                    for hh in range(hp):
                        add_ref[slot, hh, :, cs] = jnp.full((tq, LANES), NEG_INF, jnp.float32)

                @pl.when(jnp.logical_not(jnp.logical_or(far, dead)))
                def _():
                    d = qpos - kpos[:, cs]
                    idx = jnp.clip(d, 0, LANES - 1)
                    for hh in range(hp):
                        add_ref[slot, hh, :, cs] = jnp.where(
                            d >= 0, jnp.take_along_axis(tabs[hh], idx, axis=1), NEG_INF)
            return carry

        lax.fori_loop(0, n_vis - n_int, fill, 0)

    if diff:
        pl.when(batch == 0)(_build_add_tiles)

    def q_rows(ref, hh):
        q = ref[0, hh]
        if not diff:
            return q
        lane = lax.broadcasted_iota(jnp.int32, (tq, LANES), 1)
        zero = jnp.zeros_like(q)
        return jnp.concatenate([jnp.where(lane < DIFF_HEAD_DIM, q, zero),
                                jnp.where(lane >= DIFF_HEAD_DIM, q, zero)], axis=0)

    m_ref[...] = jnp.full(m_ref.shape, NEG_INF, jnp.float32)
    acc_ref[...] = jnp.zeros(acc_ref.shape, jnp.float32)

    def k_off(t):
        return pl.multiple_of(tbl_ref[i * nk + t] * tk, tk)

    mxu_n = 2 * LANES
    n_sub = tk // mxu_n

    def softmax(t, s, hh, interior):
        m_old = m_ref[hh]
        if interior:
            m_cur = jnp.max(s, axis=1, keepdims=True)
            if diff:
                m_new = jnp.maximum(m_old, m_cur + far_bias[hh])
                shift = m_new - far_bias[hh]
            else:
                m_new = jnp.maximum(m_old, m_cur)
                shift = m_new
        else:
            if diff:
                add = add_ref[t - n_int, hh]
                s = s + jnp.concatenate([add, add], axis=0)
            else:
                qpos = qpos_ref[...]
                kpos = kpos_ref[tbl_ref[i * nk + t]]
                s = jnp.concatenate(
                    [jnp.where(qpos >= kpos[:, c * LANES:(c + 1) * LANES],
                               s[:, c * LANES:(c + 1) * LANES], NEG_INF) for c in range(nc)], axis=1)
            m_new = jnp.maximum(m_old, jnp.max(s, axis=1, keepdims=True))
            shift = m_new
        m_ref[hh] = m_new
        p = jnp.concatenate([jnp.exp2(s[:, c * LANES:(c + 1) * LANES] - shift)
                             for c in range(nc)], axis=1).astype(bf)
        return p, jnp.exp2(m_old - m_new)

    def visit(t_sm, s_cur, interior, t_sc, s_nxt):
        nxt = isinstance(t_sc, str)
        off_sm = None if t_sm is None else k_off(t_sm)
        if t_sc is None or (nxt and diff):
            off_sc = None
        elif nxt:
            off_sc = pl.multiple_of(tbl_ref[i_next * nk] * tk, tk)
        else:
            off_sc = k_off(t_sc)
        sm, pvs, qrows = {}, {}, {}

        def score_dot(hh, c):
            cs = slice(c * mxu_n, (c + 1) * mxu_n)
            if hh not in qrows:
                qrows[hh] = q_rows(qn_ref if nxt else q_ref, hh)
            q = qrows[hh]
            if nxt and diff:
                k = kn_ref[0, hh, cs, :]
            else:
                k = k_ref[0, hh, pl.ds(off_sc + c * mxu_n, mxu_n), :]
            s_nxt[hh, :, cs] = lax.dot_general(q, k, NT_DIMS, preferred_element_type=jnp.float32)

        def pv_dot(hh, c):
            cs = slice(c * mxu_n, (c + 1) * mxu_n)
            v = v_ref[0, hh, pl.ds(off_sm + c * mxu_n, mxu_n), :]
            d = jnp.dot(sm[hh][0][:, cs], jnp.concatenate([v, jnp.ones_like(v)], axis=1),
                        preferred_element_type=jnp.float32)
            pvs[hh] = d if hh not in pvs else pvs[hh] + d

        tiles = [(hh, c) for hh in range(hp) for c in range(n_sub)]
        for hh, c in tiles:
            if t_sm is not None and hh not in sm:
                sm[hh] = softmax(t_sm, s_cur[hh], hh, interior)
            if t_sc is not None:
                score_dot(hh, c)
            if t_sm is not None:
                pv_dot(hh, c)
                if c == n_sub - 1:
                    alpha = sm[hh][1]
                    acc_ref[hh] = jnp.concatenate([alpha, alpha], axis=1) * acc_ref[hh] + pvs[hh]

    def step(interior):
        def body(t, carry):
            for par in (0, 1):
                @pl.when((start + t) % 2 == par)
                def _():
                    visit(t, s_refs[par], interior, t + 1, s_refs[1 - par])
            return carry
        return body

    @pl.when(first)
    def _first_scores():
        visit(None, None, False, 0, s_refs[0])

    lax.fori_loop(0, n_int, step(True), 0)
    lax.fori_loop(n_int, n_vis - 1, step(False), 0)
    for par in (0, 1):
        @pl.when((start + n_vis - 1) % 2 == par)
        def _():
            visit(n_vis - 1, s_refs[par], False, "next", s_refs[1 - par])

    if diff:
        lam = (jnp.exp(jnp.sum(lq1_ref[...] * lk1_ref[...], axis=1, keepdims=True))
               - jnp.exp(jnp.sum(lq2_ref[...] * lk2_ref[...], axis=1, keepdims=True))
               + LAMBDA_INIT)
    for hh in range(hp):
        acc = acc_ref[hh]
        o = acc[:, :LANES] / acc[:, LANES:]
        if diff:
            oo = o[:tq] - lam * o[tq:]
            o = _rms(oo, subln_ref[...]) * (1.0 - LAMBDA_INIT)
        o_ref[0, :, hh * LANES:(hh + 1) * LANES] = o.astype(o_ref.dtype)


def _lane_rep(positions):
    return jnp.broadcast_to(positions[:, None], (positions.shape[0], LANES))


def _attention(q, k, v, positions, tq, tk, hp, diff_params=None):
    B, H, S, width = q.shape
    diff = diff_params is not None
    nq, nk = S // tq, S // tk
    rows = 2 * tq if diff else tq
    tbl, n_int, n_vis = _schedule(positions, tq, tk, LANES if diff else 0)
    imap = ((lambda f: (lambda g, i, b, *_: f(g, i, b))) if diff
            else (lambda f: (lambda g, b, i, *_: f(g, i, b))))
    const = imap(lambda g, i, b: (0, 0))
    ni = lambda i: jnp.minimum(i + 1, nq - 1)
    nxt_b = lambda b: jnp.where(b == B - 1, 0, b + 1)
    nxt_i = lambda i, b: jnp.where(b == B - 1, ni(i), i)
    in_specs = [
        pl.BlockSpec((1, hp, tq, width), imap(lambda g, i, b: (b, g, i, 0))),
        pl.BlockSpec((1, hp, S, width), imap(lambda g, i, b: (b, g, 0, 0))),
        pl.BlockSpec((1, hp, S, LANES), imap(lambda g, i, b: (b, g, 0, 0))),
    ]
    operands = [q, k, v, q]
    scratch = [pltpu.VMEM((hp, rows, LANES), jnp.float32),
               pltpu.VMEM((hp, rows, 2 * LANES), jnp.float32),
               pltpu.VMEM((hp, rows, tk), jnp.float32),
               pltpu.VMEM((hp, rows, tk), jnp.float32)]
    if diff:
        in_specs += [
            pl.BlockSpec((1, hp, tq, width), imap(lambda g, i, b: (nxt_b(b), g, nxt_i(i, b), 0))),
            pl.BlockSpec((1, hp, tk, width),
                         lambda g, i, b, tbl, *_: (nxt_b(b), g, tbl[nxt_i(i, b) * nk], 0)),
        ]
        operands.append(k)
        scratch.append(pltpu.VMEM((nk, hp, tq, tk), jnp.float32))
        qt = positions.reshape(nq, tq)
        kb = positions.reshape(S // LANES, LANES)
        base = ((B * (jnp.cumsum(n_vis) - n_vis)) % 2).astype(jnp.int32)
        prefetch = [tbl, n_int, n_vis, base,
                    qt.min(axis=1), qt.max(axis=1), kb.min(axis=1), kb.max(axis=1)]
    else:
        in_specs.append(pl.BlockSpec((1, hp, tq, width), imap(lambda g, i, b: (b, g, ni(i), 0))))
        start = ((jnp.cumsum(n_vis) - n_vis) % 2).astype(jnp.int32)
        prefetch = [tbl, n_int, n_vis, start]
    in_specs += [
        pl.BlockSpec((tq, LANES), imap(lambda g, i, b: (i, 0))),
        pl.BlockSpec((nk, 1, tk), imap(lambda g, i, b: (0, 0, 0))),
    ]
    operands += [_lane_rep(positions), positions.reshape(nk, 1, tk)]
    if diff:
        rel_bias, lq1, lk1, lq2, lk2, subln = diff_params
        row = lambda a: a.reshape(1, -1).astype(jnp.float32)
        prefetch.append(rel_bias.astype(jnp.float32))
        operands += [row(lq1), row(lk1), row(lq2), row(lk2), row(subln)]
        in_specs += [pl.BlockSpec((1, DIFF_HEAD_DIM), const)] * 4 + [pl.BlockSpec((1, DIFF_V_DIM), const)]
    grid_spec = pltpu.PrefetchScalarGridSpec(
        num_scalar_prefetch=len(prefetch),
        grid=(H // hp, nq, B) if diff else (H // hp, B, nq),
        in_specs=in_specs,
        out_specs=pl.BlockSpec((1, tq, hp * LANES), imap(lambda g, i, b: (b, i, g))),
        scratch_shapes=scratch,
    )
    return pl.pallas_call(
        functools.partial(_attn_kernel, tq=tq, tk=tk, nk=nk, hp=hp, diff=diff),
        grid_spec=grid_spec,
        out_shape=jax.ShapeDtypeStruct((B, S, H * LANES), jnp.bfloat16),
        compiler_params=pltpu.CompilerParams(
            dimension_semantics=("arbitrary",) * 3, vmem_limit_bytes=VMEM_LIMIT),
        name="diff_attn" if diff else "mla_attn",
    )(*prefetch, *operands)


def _mlp_kernel(x_ref, ma_ref, mb_ref, wo_ref, g1_ref, w1_ref, w2_ref, g2_ref, o_ref, *,
                ff_chunk, row_chunk):
    bf = jnp.bfloat16
    n_a = ma_ref.shape[1]
    n_ff = D_FF // ff_chunk

    def head(rs):
        x1 = (x_ref[rs, :]
              + jnp.dot(ma_ref[rs, :], wo_ref[0:n_a, :], preferred_element_type=jnp.float32)
              + jnp.dot(mb_ref[rs, :], wo_ref[n_a:, :], preferred_element_type=jnp.float32))
        return x1, _rms(x1, g1_ref[...]).astype(bf)

    def ff(h, y, c):
        sl = slice(c * ff_chunk, (c + 1) * ff_chunk)
        a = jnp.dot(h, w1_ref[:, sl], preferred_element_type=jnp.float32)
        a = jnp.square(jnp.maximum(a, 0.0)).astype(bf)
        d = jnp.dot(a, w2_ref[sl, :], preferred_element_type=jnp.float32)
        return d if y is None else y + d

    chunks = [slice(r * row_chunk, (r + 1) * row_chunk) for r in range(x_ref.shape[0] // row_chunk)]
    x1, h = head(chunks[0])
    for r, rs in enumerate(chunks):
        y = None if r == 0 else y_first
        for c in range(0 if r == 0 else 1, n_ff - 1):
            y = ff(h, y, c)
        if r + 1 < len(chunks):
            x1_n, h_n = head(chunks[r + 1])
        y = ff(h, y, n_ff - 1)
        if r + 1 < len(chunks):
            y_first = ff(h_n, None, 0)
        o_ref[rs, :] = _rms(x1 + y, g2_ref[...])
        if r + 1 < len(chunks):
            x1, h = x1_n, h_n


def _mlp(x2d, mix_a, mix_b, w_out, g1, w1, w2, g2, tile, ff_chunk, row_chunk):
    N, D = x2d.shape
    const = lambda i: (0, 0)
    single = pl.Buffered(1)
    return pl.pallas_call(
        functools.partial(_mlp_kernel, ff_chunk=ff_chunk, row_chunk=row_chunk),
        grid=(N // tile,),
        in_specs=[
            pl.BlockSpec((tile, D), lambda i: (i, 0)),
            pl.BlockSpec((tile, mix_a.shape[1]), lambda i: (i, 0)),
            pl.BlockSpec((tile, mix_b.shape[1]), lambda i: (i, 0)),
            pl.BlockSpec(w_out.shape, const, pipeline_mode=single),
            pl.BlockSpec(g1.shape, const),
            pl.BlockSpec(w1.shape, const, pipeline_mode=single),
            pl.BlockSpec(w2.shape, const, pipeline_mode=single),
            pl.BlockSpec(g2.shape, const),
        ],
        out_specs=pl.BlockSpec((tile, D), lambda i: (i, 0)),
        out_shape=jax.ShapeDtypeStruct((N, D), jnp.float32),
        compiler_params=pltpu.CompilerParams(
            dimension_semantics=("arbitrary",), vmem_limit_bytes=VMEM_LIMIT),
        name="mlp",
    )(x2d, mix_a, mix_b, w_out, g1, w1, w2, g2)


def _swap_halves(w):
    half = w.shape[-1] // 2
    return jnp.concatenate([w[..., half:], w[..., :half]], axis=-1)


def kernel(x, positions, rel_bias, norm_attn, w_in, diff_lq1, diff_lk1, diff_lq2, diff_lk2,
           diff_subln, mla_q_norm, mla_w_uq, mla_kv_norm, mla_w_ukv, w_out, norm_mlp,
           w_mlp_in, w_mlp_out, norm_final):
    B, S, D = x.shape
    bf = jnp.bfloat16
    depth = w_in.shape[0]
    assert depth == 1
    l = 0
    row = lambda a: a.reshape(1, -1).astype(jnp.float32)

    w_in_l = w_in[l].astype(bf)
    n_diff = 2 * DIFF_QK_COLS + DIFF_V_COLS
    k_pe_cols = w_in_l[:, -MLA_ROPE_DIM:]
    w_in_x = jnp.concatenate([w_in_l[:, n_diff:], _swap_halves(k_pe_cols), w_in_l[:, :n_diff]],
                             axis=1)
    w_uq = mla_w_uq[l].astype(bf).reshape(MLA_Q_RANK, MLA_HEADS, MLA_NOPE_DIM + MLA_ROPE_DIM)
    q_pe_cols = w_uq[..., MLA_NOPE_DIM:]
    w_uq_x = jnp.concatenate([w_uq, _swap_halves(q_pe_cols)], axis=-1)
    w_uq_x = w_uq_x.reshape(MLA_Q_RANK, MLA_HEADS * MLA_QK_PAD)
    w_ukv = mla_w_ukv[l].astype(bf)

    tab = _rope_table(positions, ROPE_TILE)
    dq, dk, dv, mq, mk, mv = _proj(x, row(norm_attn[l]), w_in_x, row(mla_q_norm[l]), w_uq_x,
                                   row(mla_kv_norm[l]), w_ukv, tab, PROJ_TILE)
    mix_a = _attention(dq, dk, dv, positions, DIFF_TQ, DIFF_TK, DIFF_HEADS_PER_STEP,
                       (rel_bias, diff_lq1[l], diff_lk1[l], diff_lq2[l], diff_lk2[l], diff_subln[l]))
    mix_b = _attention(mq, mk, mv, positions, MLA_TQ, MLA_TK, MLA_HEADS_PER_STEP)
    out = _mlp(x.reshape(B * S, D), mix_a.reshape(B * S, -1), mix_b.reshape(B * S, -1),
               w_out[l].astype(bf), row(norm_mlp[l]), w_mlp_in[l].astype(bf),
               w_mlp_out[l].astype(bf), row(norm_final), MLP_TILE, MLP_FF_CHUNK, MLP_ROW_CHUNK)
    return out.reshape(B, S, D)
```

```python
import functools
import math

import jax
import jax.numpy as jnp
from jax import lax
from jax.experimental import pallas as pl
from jax.experimental.pallas import tpu as pltpu

D_MODEL = 1024
DIFF_HEADS = 4
DIFF_HEAD_DIM = 64
DIFF_V_DIM = 128
DIFF_QK_COLS = DIFF_HEADS * 2 * DIFF_HEAD_DIM
DIFF_V_COLS = DIFF_HEADS * DIFF_V_DIM
MLA_HEADS = 4
MLA_Q_RANK = 384
MLA_KV_RANK = 256
MLA_NOPE_DIM = 128
MLA_ROPE_DIM = 64
MLA_V_DIM = 128
MLA_QK_PAD = 256
ROPE_BASE = 10000.0
D_FF = 4 * D_MODEL
REL_BUCKETS = 32
REL_MAX_DIST = 128
NORM_EPS = 1e-6
NEG_INF = -1e30
LOG2E = math.log2(math.e)
LAMBDA_INIT = 0.8 - 0.6 * math.exp(-0.3 * 0)

LANES = 128
VMEM_LIMIT = 56 * 1024 * 1024

ROPE_TILE = 512
PROJ_TILE = 1024
DIFF_TQ, DIFF_TK, DIFF_HEADS_PER_STEP = 256, 512, 4
MLA_TQ, MLA_TK, MLA_HEADS_PER_STEP = 512, 512, 4
MLP_TILE, MLP_ROW_CHUNK, MLP_FF_CHUNK = 1024, 512, 512

NT_DIMS = (((1,), (1,)), ((), ()))


def _rms(x, gain):
    return x * lax.rsqrt(jnp.mean(x * x, axis=-1, keepdims=True) + NORM_EPS) * gain


def _rope_table_kernel(pos_ref, tab_ref):
    half = MLA_ROPE_DIM // 2
    lane = lax.broadcasted_iota(jnp.int32, (1, LANES), 1)
    fidx = (lane % half).astype(jnp.float32) * 2.0
    inv_freq = jnp.exp(-(fidx / MLA_ROPE_DIM) * math.log(ROPE_BASE))
    ang = pos_ref[...].astype(jnp.float32) * inv_freq
    c, s = jnp.cos(ang), jnp.sin(ang)
    tab_ref[...] = jnp.where(lane < 2 * half, c, jnp.where(lane < 3 * half, -s, s))


def _rope_table(positions, tile):
    seq = positions.shape[0]
    return pl.pallas_call(
        _rope_table_kernel,
        grid=(seq // tile,),
        in_specs=[pl.BlockSpec((tile, 1), lambda i: (i, 0))],
        out_specs=pl.BlockSpec((tile, LANES), lambda i: (i, 0)),
        out_shape=jax.ShapeDtypeStruct((seq, LANES), jnp.float32),
        name="rope_table",
    )(positions.reshape(seq, 1))


def _rope_slab(slab, tab):
    prod = slab * tab
    return prod + pltpu.roll(prod, MLA_ROPE_DIM, axis=1)


def _proj_kernel(x_ref, g_ref, w_in_ref, qn_ref, w_uq_ref, kvn_ref, w_ukv_ref, tab_ref,
                 dq_ref, dk_ref, dv_ref, mq_ref, mk_ref, mv_ref):
    bf = jnp.bfloat16
    x = x_ref[0]
    h = (x * g_ref[...]).astype(bf)
    r = lax.rsqrt(jnp.mean(x * x, axis=-1, keepdims=True) + NORM_EPS)
    o_ckv = MLA_Q_RANK
    o_kpe = o_ckv + MLA_KV_RANK
    o_dq = o_kpe + LANES
    o_dk = o_dq + DIFF_QK_COLS
    o_dv = o_dk + DIFF_QK_COLS
    p = jnp.dot(h, w_in_ref[:, :o_dq], preferred_element_type=jnp.float32) * r
    tab = tab_ref[...]
    lane = lax.broadcasted_iota(jnp.int32, tab.shape, 1)

    dscale = (DIFF_HEAD_DIM ** -0.5) * LOG2E
    pq = jnp.dot(h, w_in_ref[:, o_dq:o_dk], preferred_element_type=jnp.float32)
    rq = r * dscale
    for hh in range(DIFF_HEADS):
        dq_ref[0, hh] = (pq[:, hh * LANES:(hh + 1) * LANES] * rq).astype(bf)

    cq = _rms(p[:, :o_ckv], qn_ref[...]).astype(bf)
    q = jnp.dot(cq, w_uq_ref[...], preferred_element_type=jnp.float32)
    ckv = _rms(p[:, o_ckv:o_kpe], kvn_ref[...]).astype(bf)
    kv = jnp.dot(ckv, w_ukv_ref[...], preferred_element_type=jnp.float32)
    k_rope = _rope_slab(p[:, o_kpe:o_dq], tab).astype(bf)

    pkv = jnp.dot(h, w_in_ref[:, o_dk:], preferred_element_type=jnp.float32)
    for hh in range(DIFF_HEADS):
        dk_ref[0, hh] = (pkv[:, hh * LANES:(hh + 1) * LANES] * r).astype(bf)
        dv_ref[0, hh] = (pkv[:, DIFF_QK_COLS + hh * LANES:DIFF_QK_COLS + (hh + 1) * LANES]
                         * r).astype(bf)

    mscale = ((MLA_NOPE_DIM + MLA_ROPE_DIM) ** -0.5) * LOG2E
    for hh in range(MLA_HEADS):
        base = hh * MLA_QK_PAD
        q_rope = _rope_slab(q[:, base + LANES:base + 2 * LANES], tab)
        q_rope = jnp.where(lane < MLA_ROPE_DIM, q_rope, 0.0)
        mq_ref[0, hh, :, 0:LANES] = (q[:, base:base + LANES] * mscale).astype(bf)
        mq_ref[0, hh, :, LANES:2 * LANES] = (q_rope * mscale).astype(bf)
        mk_ref[0, hh, :, 0:LANES] = kv[:, base:base + LANES].astype(bf)
        mk_ref[0, hh, :, LANES:2 * LANES] = k_rope
        mv_ref[0, hh] = kv[:, base + LANES:base + 2 * LANES].astype(bf)


def _proj(x, g, w_in, qn, w_uq, kvn, w_ukv, tab, tile):
    B, S, D = x.shape
    bf = jnp.bfloat16
    const = lambda b, i: (0, 0)
    head_out = lambda w: pl.BlockSpec((1, DIFF_HEADS, tile, w), lambda b, i: (b, 0, i, 0))
    shp = lambda w: jax.ShapeDtypeStruct((B, DIFF_HEADS, S, w), bf)
    return pl.pallas_call(
        _proj_kernel,
        grid=(B, S // tile),
        in_specs=[
            pl.BlockSpec((1, tile, D), lambda b, i: (b, i, 0)),
            pl.BlockSpec(g.shape, const),
            pl.BlockSpec(w_in.shape, const),
            pl.BlockSpec(qn.shape, const),
            pl.BlockSpec(w_uq.shape, const),
            pl.BlockSpec(kvn.shape, const),
            pl.BlockSpec(w_ukv.shape, const),
            pl.BlockSpec((tile, LANES), lambda b, i: (i, 0)),
        ],
        out_specs=[head_out(LANES), head_out(LANES), head_out(LANES),
                   head_out(MLA_QK_PAD), head_out(MLA_QK_PAD), head_out(LANES)],
        out_shape=[shp(LANES), shp(LANES), shp(LANES),
                   shp(MLA_QK_PAD), shp(MLA_QK_PAD), shp(LANES)],
        compiler_params=pltpu.CompilerParams(
            dimension_semantics=("arbitrary", "arbitrary"), vmem_limit_bytes=VMEM_LIMIT),
        name="proj",
    )(x, g, w_in, qn, w_uq, kvn, w_ukv, tab)


def _schedule(positions, tq, tk, far):
    seq = positions.shape[0]
    qp = positions.reshape(seq // tq, tq)
    kp = positions.reshape(seq // tk, tk)
    qmin, qmax = qp.min(axis=1), qp.max(axis=1)
    kmin, kmax = kp.min(axis=1), kp.max(axis=1)
    skip = kmin[None, :] > qmax[:, None]
    interior = (qmin[:, None] - kmax[None, :]) >= far
    cls = jnp.where(interior, 0, jnp.where(skip, 2, 1)).astype(jnp.int32)
    order = jnp.argsort(cls, axis=1, stable=True).astype(jnp.int32)
    n_int = jnp.sum(cls == 0, axis=1).astype(jnp.int32)
    n_vis = jnp.sum(cls < 2, axis=1).astype(jnp.int32)
    return order.reshape(-1), n_int, n_vis


def _t5_bias_row(relb_ref, head):
    n = lax.broadcasted_iota(jnp.int32, (1, LANES), 1)
    max_exact = REL_BUCKETS // 2
    nf = jnp.maximum(n, 1).astype(jnp.float32)
    large = max_exact + jnp.floor(jnp.log(nf / max_exact) / math.log(REL_MAX_DIST / max_exact)
                                  * (REL_BUCKETS - max_exact)).astype(jnp.int32)
    large = jnp.minimum(large, REL_BUCKETS - 1)
    bucket = jnp.where(n < max_exact, n, large)
    row = jnp.zeros((1, LANES), jnp.float32)
    for b in range(REL_BUCKETS):
        row = jnp.where(bucket == b, relb_ref[b, head] * LOG2E, row)
    return row


def _attn_kernel(*refs, tq, tk, nk, hp, diff):
    if diff:
        (tbl_ref, nint_ref, nvis_ref, base_ref, qmin_ref, qmax_ref, kbmin_ref, kbmax_ref, relb_ref,
         q_ref, k_ref, v_ref, qn_ref, kn_ref,
         qpos_ref, kpos_ref, lq1_ref, lk1_ref, lq2_ref, lk2_ref, subln_ref,
         o_ref, m_ref, acc_ref, s0_ref, s1_ref, add_ref) = refs
        grp, i, batch = pl.program_id(0), pl.program_id(1), pl.program_id(2)
        n_vis = nvis_ref[i]
        first = jnp.logical_and(i == 0, batch == 0)
        start = (base_ref[i] + batch * n_vis) % 2
    else:
        (tbl_ref, nint_ref, nvis_ref, start_ref, q_ref, k_ref, v_ref, qn_ref, qpos_ref, kpos_ref,
         o_ref, m_ref, acc_ref, s0_ref, s1_ref) = refs
        grp, batch, i = pl.program_id(0), pl.program_id(1), pl.program_id(2)
        n_vis = nvis_ref[i]
        first = i == 0
        start = start_ref[i]
        i_next = jnp.minimum(i + 1, pl.num_programs(2) - 1)
    s_refs = (s0_ref, s1_ref)
    n_int = jnp.minimum(nint_ref[i], n_vis - 1)
    nc = tk // LANES
    bf = jnp.bfloat16

    if diff:
        far_bias = [relb_ref[REL_BUCKETS - 1, grp * hp + hh] * LOG2E for hh in range(hp)]

    def _build_add_tiles():
        qpos = qpos_ref[...]
        tabs = [jnp.broadcast_to(_t5_bias_row(relb_ref, grp * hp + hh), (tq, LANES))
                for hh in range(hp)]

        def fill(slot, carry):
            j = tbl_ref[i * nk + n_int + slot]
            kpos = kpos_ref[j]
            for c in range(nc):
                cs = slice(c * LANES, (c + 1) * LANES)
                blk = j * nc + c
                far = qmin_ref[i] - kbmax_ref[blk] >= LANES - 1
                dead = kbmin_ref[blk] > qmax_ref[i]

                @pl.when(far)
                def _():
                    for hh in range(hp):
                        add_ref[slot, hh, :, cs] = jnp.full((tq, LANES), far_bias[hh], jnp.float32)

                @pl.when(dead)
                def _():
                    for hh in range(hp):
                        add_ref[slot, hh, :, cs] = jnp.full((tq, LANES), NEG_INF, jnp.float32)

                @pl.when(jnp.logical_not(jnp.logical_or(far, dead)))
                def _():
                    d = qpos - kpos[:, cs]
                    idx = jnp.clip(d, 0, LANES - 1)
                    for hh in range(hp):
                        add_ref[slot, hh, :, cs] = jnp.where(
                            d >= 0, jnp.take_along_axis(tabs[hh], idx, axis=1), NEG_INF)
            return carry

        lax.fori_loop(0, n_vis - n_int, fill, 0)

    if diff:
        pl.when(batch == 0)(_build_add_tiles)

    def q_rows(ref, hh):
        q = ref[0, hh]
        if not diff:
            return q
        lane = lax.broadcasted_iota(jnp.int32, (tq, LANES), 1)
        zero = jnp.zeros_like(q)
        return jnp.concatenate([jnp.where(lane < DIFF_HEAD_DIM, q, zero),
                                jnp.where(lane >= DIFF_HEAD_DIM, q, zero)], axis=0)

    m_ref[...] = jnp.full(m_ref.shape, NEG_INF, jnp.float32)
    acc_ref[...] = jnp.zeros(acc_ref.shape, jnp.float32)

    def k_off(t):
        return pl.multiple_of(tbl_ref[i * nk + t] * tk, tk)

    mxu_n = 2 * LANES
    n_sub = tk // mxu_n

    def softmax(t, s, hh, interior):
        m_old = m_ref[hh]
        if interior:
            m_cur = jnp.max(s, axis=1, keepdims=True)
            if diff:
                m_new = jnp.maximum(m_old, m_cur + far_bias[hh])
                shift = m_new - far_bias[hh]
            else:
                m_new = jnp.maximum(m_old, m_cur)
                shift = m_new
        else:
            if diff:
                add = add_ref[t - n_int, hh]
                s = s + jnp.concatenate([add, add], axis=0)
            else:
                qpos = qpos_ref[...]
                kpos = kpos_ref[tbl_ref[i * nk + t]]
                s = jnp.concatenate(
                    [jnp.where(qpos >= kpos[:, c * LANES:(c + 1) * LANES],
                               s[:, c * LANES:(c + 1) * LANES], NEG_INF) for c in range(nc)], axis=1)
            m_new = jnp.maximum(m_old, jnp.max(s, axis=1, keepdims=True))
            shift = m_new
        m_ref[hh] = m_new
        p = jnp.concatenate([jnp.exp2(s[:, c * LANES:(c + 1) * LANES] - shift)
                             for c in range(nc)], axis=1).astype(bf)
        return p, jnp.exp2(m_old - m_new)

    def visit(t_sm, s_cur, interior, t_sc, s_nxt):
        nxt = isinstance(t_sc, str)
        off_sm = None if t_sm is None else k_off(t_sm)
        if t_sc is None or (nxt and diff):
            off_sc = None
        elif nxt:
            off_sc = pl.multiple_of(tbl_ref[i_next * nk] * tk, tk)
        else:
            off_sc = k_off(t_sc)
        sm, pvs, qrows = {}, {}, {}

        def score_dot(hh, c):
            cs = slice(c * mxu_n, (c + 1) * mxu_n)
            if hh not in qrows:
                qrows[hh] = q_rows(qn_ref if nxt else q_ref, hh)
            q = qrows[hh]
            if nxt and diff:
                k = kn_ref[0, hh, cs, :]
            else:
                k = k_ref[0, hh, pl.ds(off_sc + c * mxu_n, mxu_n), :]
            s_nxt[hh, :, cs] = lax.dot_general(q, k, NT_DIMS, preferred_element_type=jnp.float32)

        def pv_dot(hh, c):
            cs = slice(c * mxu_n, (c + 1) * mxu_n)
            v = v_ref[0, hh, pl.ds(off_sm + c * mxu_n, mxu_n), :]
            d = jnp.dot(sm[hh][0][:, cs], jnp.concatenate([v, jnp.ones_like(v)], axis=1),
                        preferred_element_type=jnp.float32)
            pvs[hh] = d if hh not in pvs else pvs[hh] + d

        tiles = [(hh, c) for hh in range(hp) for c in range(n_sub)]
        for hh, c in tiles:
            if t_sm is not None and hh not in sm:
                sm[hh] = softmax(t_sm, s_cur[hh], hh, interior)
            if t_sc is not None:
                score_dot(hh, c)
            if t_sm is not None:
                pv_dot(hh, c)
                if c == n_sub - 1:
                    alpha = sm[hh][1]
                    acc_ref[hh] = jnp.concatenate([alpha, alpha], axis=1) * acc_ref[hh] + pvs[hh]

    def step(interior):
        def body(t, carry):
            for par in (0, 1):
                @pl.when((start + t) % 2 == par)
                def _():
                    visit(t, s_refs[par], interior, t + 1, s_refs[1 - par])
            return carry
        return body

    @pl.when(first)
    def _first_scores():
        visit(None, None, False, 0, s_refs[0])

    lax.fori_loop(0, n_int, step(True), 0)
    lax.fori_loop(n_int, n_vis - 1, step(False), 0)
    for par in (0, 1):
        @pl.when((start + n_vis - 1) % 2 == par)
        def _():
            visit(n_vis - 1, s_refs[par], False, "next", s_refs[1 - par])

    if diff:
        lam = (jnp.exp(jnp.sum(lq1_ref[...] * lk1_ref[...], axis=1, keepdims=True))
               - jnp.exp(jnp.sum(lq2_ref[...] * lk2_ref[...], axis=1, keepdims=True))
               + LAMBDA_INIT)
    for hh in range(hp):
        acc = acc_ref[hh]
        o = acc[:, :LANES] / acc[:, LANES:]
        if diff:
            oo = o[:tq] - lam * o[tq:]
            o = _rms(oo, subln_ref[...]) * (1.0 - LAMBDA_INIT)
        o_ref[0, :, hh * LANES:(hh + 1) * LANES] = o.astype(o_ref.dtype)


def _lane_rep(positions):
    return jnp.broadcast_to(positions[:, None], (positions.shape[0], LANES))


def _attention(q, k, v, positions, tq, tk, hp, diff_params=None):
    B, H, S, width = q.shape
    diff = diff_params is not None
    nq, nk = S // tq, S // tk
    rows = 2 * tq if diff else tq
    tbl, n_int, n_vis = _schedule(positions, tq, tk, LANES if diff else 0)
    imap = ((lambda f: (lambda g, i, b, *_: f(g, i, b))) if diff
            else (lambda f: (lambda g, b, i, *_: f(g, i, b))))
    const = imap(lambda g, i, b: (0, 0))
    ni = lambda i: jnp.minimum(i + 1, nq - 1)
    nxt_b = lambda b: jnp.where(b == B - 1, 0, b + 1)
    nxt_i = lambda i, b: jnp.where(b == B - 1, ni(i), i)
    in_specs = [
        pl.BlockSpec((1, hp, tq, width), imap(lambda g, i, b: (b, g, i, 0))),
        pl.BlockSpec((1, hp, S, width), imap(lambda g, i, b: (b, g, 0, 0))),
        pl.BlockSpec((1, hp, S, LANES), imap(lambda g, i, b: (b, g, 0, 0))),
    ]
    operands = [q, k, v, q]
    scratch = [pltpu.VMEM((hp, rows, LANES), jnp.float32),
               pltpu.VMEM((hp, rows, 2 * LANES), jnp.float32),
               pltpu.VMEM((hp, rows, tk), jnp.float32),
               pltpu.VMEM((hp, rows, tk), jnp.float32)]
    if diff:
        in_specs += [
            pl.BlockSpec((1, hp, tq, width), imap(lambda g, i, b: (nxt_b(b), g, nxt_i(i, b), 0))),
            pl.BlockSpec((1, hp, tk, width),
                         lambda g, i, b, tbl, *_: (nxt_b(b), g, tbl[nxt_i(i, b) * nk], 0)),
        ]
        operands.append(k)
        scratch.append(pltpu.VMEM((nk, hp, tq, tk), jnp.float32))
        qt = positions.reshape(nq, tq)
        kb = positions.reshape(S // LANES, LANES)
        base = ((B * (jnp.cumsum(n_vis) - n_vis)) % 2).astype(jnp.int32)
        prefetch = [tbl, n_int, n_vis, base,
                    qt.min(axis=1), qt.max(axis=1), kb.min(axis=1), kb.max(axis=1)]
    else:
        in_specs.append(pl.BlockSpec((1, hp, tq, width), imap(lambda g, i, b: (b, g, ni(i), 0))))
        start = ((jnp.cumsum(n_vis) - n_vis) % 2).astype(jnp.int32)
        prefetch = [tbl, n_int, n_vis, start]
    in_specs += [
        pl.BlockSpec((tq, LANES), imap(lambda g, i, b: (i, 0))),
        pl.BlockSpec((nk, 1, tk), imap(lambda g, i, b: (0, 0, 0))),
    ]
    operands += [_lane_rep(positions), positions.reshape(nk, 1, tk)]
    if diff:
        rel_bias, lq1, lk1, lq2, lk2, subln = diff_params
        row = lambda a: a.reshape(1, -1).astype(jnp.float32)
        prefetch.append(rel_bias.astype(jnp.float32))
        operands += [row(lq1), row(lk1), row(lq2), row(lk2), row(subln)]
        in_specs += [pl.BlockSpec((1, DIFF_HEAD_DIM), const)] * 4 + [pl.BlockSpec((1, DIFF_V_DIM), const)]
    grid_spec = pltpu.PrefetchScalarGridSpec(
        num_scalar_prefetch=len(prefetch),
        grid=(H // hp, nq, B) if diff else (H // hp, B, nq),
        in_specs=in_specs,
        out_specs=pl.BlockSpec((1, tq, hp * LANES), imap(lambda g, i, b: (b, i, g))),
        scratch_shapes=scratch,
    )
    return pl.pallas_call(
        functools.partial(_attn_kernel, tq=tq, tk=tk, nk=nk, hp=hp, diff=diff),
        grid_spec=grid_spec,
        out_shape=jax.ShapeDtypeStruct((B, S, H * LANES), jnp.bfloat16),
        compiler_params=pltpu.CompilerParams(
            dimension_semantics=("arbitrary",) * 3, vmem_limit_bytes=VMEM_LIMIT),
        name="diff_attn" if diff else "mla_attn",
    )(*prefetch, *operands)


def _mlp_kernel(x_ref, ma_ref, mb_ref, wo_ref, g1_ref, w1_ref, w2_ref, g2_ref, o_ref, *,
                ff_chunk, row_chunk):
    bf = jnp.bfloat16
    n_a = ma_ref.shape[1]
    n_ff = D_FF // ff_chunk

    def head(rs):
        x1 = (x_ref[rs, :]
              + jnp.dot(ma_ref[rs, :], wo_ref[0:n_a, :], preferred_element_type=jnp.float32)
              + jnp.dot(mb_ref[rs, :], wo_ref[n_a:, :], preferred_element_type=jnp.float32))
        return x1, _rms(x1, g1_ref[...]).astype(bf)

    def ff(h, y, c):
        sl = slice(c * ff_chunk, (c + 1) * ff_chunk)
        a = jnp.dot(h, w1_ref[:, sl], preferred_element_type=jnp.float32)
        a = jnp.square(jnp.maximum(a, 0.0)).astype(bf)
        d = jnp.dot(a, w2_ref[sl, :], preferred_element_type=jnp.float32)
        return d if y is None else y + d

    chunks = [slice(r * row_chunk, (r + 1) * row_chunk) for r in range(x_ref.shape[0] // row_chunk)]
    x1, h = head(chunks[0])
    for r, rs in enumerate(chunks):
        y = None if r == 0 else y_first
        for c in range(0 if r == 0 else 1, n_ff - 1):
            y = ff(h, y, c)
        if r + 1 < len(chunks):
            x1_n, h_n = head(chunks[r + 1])
        y = ff(h, y, n_ff - 1)
        if r + 1 < len(chunks):
            y_first = ff(h_n, None, 0)
        o_ref[rs, :] = _rms(x1 + y, g2_ref[...])
        if r + 1 < len(chunks):
            x1, h = x1_n, h_n


def _mlp(x2d, mix_a, mix_b, w_out, g1, w1, w2, g2, tile, ff_chunk, row_chunk):
    N, D = x2d.shape
    const = lambda i: (0, 0)
    single = pl.Buffered(1)
    return pl.pallas_call(
        functools.partial(_mlp_kernel, ff_chunk=ff_chunk, row_chunk=row_chunk),
        grid=(N // tile,),
        in_specs=[
            pl.BlockSpec((tile, D), lambda i: (i, 0)),
            pl.BlockSpec((tile, mix_a.shape[1]), lambda i: (i, 0)),
            pl.BlockSpec((tile, mix_b.shape[1]), lambda i: (i, 0)),
            pl.BlockSpec(w_out.shape, const, pipeline_mode=single),
            pl.BlockSpec(g1.shape, const),
            pl.BlockSpec(w1.shape, const, pipeline_mode=single),
            pl.BlockSpec(w2.shape, const, pipeline_mode=single),
            pl.BlockSpec(g2.shape, const),
        ],
        out_specs=pl.BlockSpec((tile, D), lambda i: (i, 0)),
        out_shape=jax.ShapeDtypeStruct((N, D), jnp.float32),
        compiler_params=pltpu.CompilerParams(
            dimension_semantics=("arbitrary",), vmem_limit_bytes=VMEM_LIMIT),
        name="mlp",
    )(x2d, mix_a, mix_b, w_out, g1, w1, w2, g2)


def _swap_halves(w):
    half = w.shape[-1] // 2
    return jnp.concatenate([w[..., half:], w[..., :half]], axis=-1)


def kernel(x, positions, rel_bias, norm_attn, w_in, diff_lq1, diff_lk1, diff_lq2, diff_lk2,
           diff_subln, mla_q_norm, mla_w_uq, mla_kv_norm, mla_w_ukv, w_out, norm_mlp,
           w_mlp_in, w_mlp_out, norm_final):
    B, S, D = x.shape
    bf = jnp.bfloat16
    depth = w_in.shape[0]
    assert depth == 1
    l = 0
    row = lambda a: a.reshape(1, -1).astype(jnp.float32)

    w_in_l = w_in[l].astype(bf)
    n_diff = 2 * DIFF_QK_COLS + DIFF_V_COLS
    k_pe_cols = w_in_l[:, -MLA_ROPE_DIM:]
    w_in_x = jnp.concatenate([w_in_l[:, n_diff:], _swap_halves(k_pe_cols), w_in_l[:, :n_diff]],
                             axis=1)
    w_uq = mla_w_uq[l].astype(bf).reshape(MLA_Q_RANK, MLA_HEADS, MLA_NOPE_DIM + MLA_ROPE_DIM)
    q_pe_cols = w_uq[..., MLA_NOPE_DIM:]
    w_uq_x = jnp.concatenate([w_uq, _swap_halves(q_pe_cols)], axis=-1)
    w_uq_x = w_uq_x.reshape(MLA_Q_RANK, MLA_HEADS * MLA_QK_PAD)
    w_ukv = mla_w_ukv[l].astype(bf)

    tab = _rope_table(positions, ROPE_TILE)
    dq, dk, dv, mq, mk, mv = _proj(x, row(norm_attn[l]), w_in_x, row(mla_q_norm[l]), w_uq_x,
                                   row(mla_kv_norm[l]), w_ukv, tab, PROJ_TILE)
    mix_a = _attention(dq, dk, dv, positions, DIFF_TQ, DIFF_TK, DIFF_HEADS_PER_STEP,
                       (rel_bias, diff_lq1[l], diff_lk1[l], diff_lq2[l], diff_lk2[l], diff_subln[l]))
    mix_b = _attention(mq, mk, mv, positions, MLA_TQ, MLA_TK, MLA_HEADS_PER_STEP)
    out = _mlp(x.reshape(B * S, D), mix_a.reshape(B * S, -1), mix_b.reshape(B * S, -1),
               w_out[l].astype(bf), row(norm_mlp[l]), w_mlp_in[l].astype(bf),
               w_mlp_out[l].astype(bf), row(norm_final), MLP_TILE, MLP_FF_CHUNK, MLP_ROW_CHUNK)
    return out.reshape(B, S, D)
```
